```python
import functools
import jax
import jax.numpy as jnp
from jax import lax
import numpy as np

D_MODEL = 1024
BATCH = 2
SEQ = 16384
DEPTH = 1
DEC_BATCH = 128
DEC_SEQ = 8
PAST_LEN = 8192
PAGE_SIZE = 128

A_HEADS = 8
A_HEAD_DIM = 64
A_WIDTH = A_HEADS * A_HEAD_DIM
MOBA_BLOCK = 256
MOBA_TOPK = 3
Q_BLOCK = 128
ROPE_THETA = 10000.0
B_HEADS = 4
B_KEY_DIM = D_MODEL // (2 * B_HEADS)
B_VAL_DIM = D_MODEL // B_HEADS
B_KEY_WIDTH = B_HEADS * B_KEY_DIM
B_VAL_WIDTH = B_HEADS * B_VAL_DIM
GATE_RANK = 16
GATE_NORMALIZER = 16.0
GLA_CHUNK = 64
N_EXPERTS = 64
N_GROUPS = 8
TOPK_GROUPS = 4
TOP_K = 8
EXPERT_DIM = D_MODEL // 4
SHARED_DIM = D_MODEL // 4
ROUTED_SCALE = 2.5
MOE_ROWS = 128

EPS = 1e-6
NEG = -1e30
IN_SIZES = (A_WIDTH, A_WIDTH, A_WIDTH, B_KEY_WIDTH, B_KEY_WIDTH, B_VAL_WIDTH, B_VAL_WIDTH, GATE_RANK, D_MODEL, D_MODEL)
IN_WIDTH = 3 * A_WIDTH + 2 * B_KEY_WIDTH + 2 * B_VAL_WIDTH + GATE_RANK + 2 * D_MODEL

kernel_name = 'moba_gla_moe_adaln_hybrid_step'


def rmsnorm(x, w):
    xf = x.astype(jnp.float32)
    y = xf * lax.rsqrt(jnp.mean(xf * xf, axis=-1, keepdims=True) + EPS)
    return (y * w.astype(jnp.float32)).astype(x.dtype)


def rope(x, pos):
    d = x.shape[-1]
    inv_freq = ROPE_THETA ** (-jnp.arange(0, d, 2, dtype=jnp.float32) / d)
    ang = pos.astype(jnp.float32)[:, None] * inv_freq[None, :]
    cos = jnp.concatenate([jnp.cos(ang), jnp.cos(ang)], axis=-1)[None, :, None, :]
    sin = jnp.concatenate([jnp.sin(ang), jnp.sin(ang)], axis=-1)[None, :, None, :]
    xf = x.astype(jnp.float32)
    x1, x2 = jnp.split(xf, 2, axis=-1)
    rot = jnp.concatenate([-x2, x1], axis=-1)
    return (xf * cos + rot * sin).astype(x.dtype)


def adaln_mod(c, w_ada, b_ada):
    m = jax.nn.silu(c) @ w_ada + b_ada
    return jnp.split(m[:, None, :], 6, axis=-1)


def mixer_inputs(h, pos, w_in, w_gate_up, b_gate):
    n, t = h.shape[:2]
    split_at = [int(i) for i in np.cumsum(IN_SIZES)[:-1]]
    qa, ka, va, qb, kb, vb, gb, gr, gate_a, gate_b = jnp.split(h @ w_in, split_at, axis=-1)
    qa = rope(qa.reshape(n, t, A_HEADS, A_HEAD_DIM), pos)
    ka = rope(ka.reshape(n, t, A_HEADS, A_HEAD_DIM), pos)
    va = va.reshape(n, t, A_HEADS, A_HEAD_DIM)
    qb = qb.reshape(n, t, B_HEADS, B_KEY_DIM) * (B_KEY_DIM ** -0.5)
    kb = kb.reshape(n, t, B_HEADS, B_KEY_DIM)
    vb = vb.reshape(n, t, B_HEADS, B_VAL_DIM)
    log_a = jax.nn.log_sigmoid((gr @ w_gate_up + b_gate).astype(jnp.float32)) / GATE_NORMALIZER
    log_a = log_a.reshape(n, t, B_HEADS, B_KEY_DIM)
    return qa, ka, va, qb, kb, vb, log_a, gb, gate_a, gate_b


def mixer_output(oa, ob, gb, gate_a, gate_b, gla_norm_w, w_proj_a, w_proj_b, w_out):
    n, t = oa.shape[:2]
    ob = rmsnorm(ob, gla_norm_w).reshape(n, t, B_VAL_WIDTH) * jax.nn.silu(gb)
    oa = oa.reshape(n, t, A_WIDTH)
    merged = jax.nn.sigmoid(gate_a) * (oa @ w_proj_a) + jax.nn.sigmoid(gate_b) * (ob @ w_proj_b)
    return merged @ w_out


def moba_attend(qc, ksel, vsel, valid, kown, vown, own_mask):
    scale = A_HEAD_DIM ** -0.5
    lo = jnp.einsum('nqhd,nkhd->nqhk', qc, kown, preferred_element_type=jnp.float32) * scale
    lo = jnp.where(own_mask[None, :, None, :], lo, NEG)
    if ksel is None:
        p = jax.nn.softmax(lo, axis=-1)
        out = jnp.einsum('nqhk,nkhd->nqhd', p.astype(vown.dtype), vown)
        return out.astype(qc.dtype)
    ls = jnp.einsum('nqhd,nqhjkd->nqhjk', qc, ksel, preferred_element_type=jnp.float32) * scale
    ls = jnp.where(valid[..., None], ls, NEG)
    n_s = ls.shape[3] * ls.shape[4]
    logits = jnp.concatenate([ls.reshape(ls.shape[:3] + (n_s,)), lo], axis=-1)
    p = jax.nn.softmax(logits, axis=-1)
    p_sel = p[..., :n_s].reshape(ksel.shape[:5]).astype(vsel.dtype)
    p_own = p[..., n_s:].astype(vown.dtype)
    out = jnp.einsum('nqhjk,nqhjkd->nqhd', p_sel, vsel) + jnp.einsum('nqhk,nkhd->nqhd', p_own, vown)
    return out.astype(qc.dtype)


def moba_prompt(q, k, v):
    nb_, s, h, d = q.shape
    n_blk = -(-s // MOBA_BLOCK)
    n_full = s // MOBA_BLOCK
    n_sel = min(MOBA_TOPK, n_blk - 1)
    pad = n_blk * MOBA_BLOCK - s
    kp = jnp.pad(k, ((0, 0), (0, pad), (0, 0), (0, 0)))
    vp = jnp.pad(v, ((0, 0), (0, pad), (0, 0), (0, 0)))
    if n_sel > 0:
        kblocks = kp[:, :n_full * MOBA_BLOCK].reshape(nb_, n_full, MOBA_BLOCK, h, d)
        vblocks = vp[:, :n_full * MOBA_BLOCK].reshape(nb_, n_full, MOBA_BLOCK, h, d)
        kmean = jnp.mean(kblocks, axis=2, dtype=jnp.float32).astype(k.dtype)
    bidx = jnp.arange(nb_)[:, None, None, None]
    hidx = jnp.arange(h)[None, None, :, None]

    def one_query_block(c):
        q0 = c * Q_BLOCK
        qc = lax.dynamic_slice_in_dim(q, q0, Q_BLOCK, axis=1)
        own = q0 // MOBA_BLOCK
        qpos = q0 + jnp.arange(Q_BLOCK)
        kown = lax.dynamic_slice_in_dim(kp, own * MOBA_BLOCK, MOBA_BLOCK, axis=1)
        vown = lax.dynamic_slice_in_dim(vp, own * MOBA_BLOCK, MOBA_BLOCK, axis=1)
        own_mask = (own * MOBA_BLOCK + jnp.arange(MOBA_BLOCK))[None, :] <= qpos[:, None]
        ksel = vsel = valid = None
        if n_sel > 0:
            sc = jnp.einsum('bqhd,bnhd->bqhn', qc, kmean, preferred_element_type=jnp.float32)
            sc = jnp.where(jnp.arange(n_full) < own, sc, NEG)
            _, sel = lax.top_k(sc, n_sel)
            valid = sel < own
            ksel = kblocks[bidx, sel, :, hidx]
            vsel = vblocks[bidx, sel, :, hidx]
        return moba_attend(qc, ksel, vsel, valid, kown, vown, own_mask)

    out = lax.map(one_query_block, jnp.arange(s // Q_BLOCK))
    return out.transpose(1, 0, 2, 3, 4).reshape(nb_, s, h, d)


def moba_sample(q, k, v, cache_k, cache_v, page_table, layer):
    db, t, h, d = q.shape
    ppb = MOBA_BLOCK // PAGE_SIZE
    n_full = PAST_LEN // MOBA_BLOCK
    n_tail_pages = PAST_LEN // PAGE_SIZE - n_full * ppb
    n_sel = min(MOBA_TOPK, n_full)
    tail_pages = page_table[:, n_full * ppb:]
    k_tail = cache_k[layer, tail_pages].reshape(db, n_tail_pages * PAGE_SIZE, h, d)
    v_tail = cache_v[layer, tail_pages].reshape(db, n_tail_pages * PAGE_SIZE, h, d)
    kown = jnp.concatenate([k_tail, k], axis=1)
    vown = jnp.concatenate([v_tail, v], axis=1)
    own_mask = jnp.concatenate([jnp.ones((t, n_tail_pages * PAGE_SIZE), bool), jnp.tril(jnp.ones((t, t), bool))], axis=1)
    ksel = vsel = valid = None
    if n_sel > 0:
        full_pages = page_table[:, :n_full * ppb]
        kmean = jnp.mean(cache_k[layer, full_pages], axis=2, dtype=jnp.float32)
        kmean = kmean.reshape(db, n_full, ppb, h, d).mean(axis=2).astype(q.dtype)
        sc = jnp.einsum('bqhd,bnhd->bqhn', q, kmean, preferred_element_type=jnp.float32)
        _, sel = lax.top_k(sc, n_sel)
        valid = jnp.ones(sel.shape, bool)
        logical = sel[..., None] * ppb + jnp.arange(ppb)
        phys = page_table[jnp.arange(db)[:, None, None, None, None], logical]
        hidx = jnp.arange(h)[None, None, :, None, None]
        ksel = cache_k[layer, phys, :, hidx].reshape(db, t, h, n_sel, MOBA_BLOCK, d)
        vsel = cache_v[layer, phys, :, hidx].reshape(db, t, h, n_sel, MOBA_BLOCK, d)
    return moba_attend(q, ksel, vsel, valid, kown, vown, own_mask)


def to_chunks(x, c):
    n, t, h, d = x.shape
    return x.reshape(n, t // c, c, h, d).transpose(1, 0, 3, 2, 4)


def gla_chunked(q, k, v, log_a, s0, c):
    n, t, h, dk = q.shape
    dv = v.shape[-1]
    causal = jnp.tril(jnp.ones((c, c), bool))

    def step(state, inp):
        qc, kc, vc, gc = inp
        b = jnp.cumsum(gc, axis=2)
        b_last = b[:, :, -1]
        inter = jnp.einsum('nhtd,nhdv->nhtv', qc * jnp.exp(b), state)
        rel = jnp.where(causal[:, :, None], b[:, :, :, None, :] - b[:, :, None, :, :], -jnp.inf)
        scores = jnp.einsum('nhtd,nhsd,nhtsd->nhts', qc, kc, jnp.exp(rel))
        intra = jnp.einsum('nhts,nhsv->nhtv', scores, vc)
        new_state = state * jnp.exp(b_last)[..., None] + jnp.einsum('nhsd,nhsv->nhdv', kc * jnp.exp(b_last[:, :, None] - b), vc)
        return new_state, inter + intra

    xs = (to_chunks(q.astype(jnp.float32), c), to_chunks(k.astype(jnp.float32), c),
          to_chunks(v.astype(jnp.float32), c), to_chunks(log_a.astype(jnp.float32), c))
    state, o = lax.scan(step, s0.astype(jnp.float32), xs)
    o = o.transpose(1, 0, 3, 2, 4).reshape(n, t, h, dv)
    return o.astype(v.dtype), state.astype(s0.dtype)


def swiglu(x, w1, w3, w2):
    return (jax.nn.silu(x @ w1) * (x @ w3)) @ w2


def moe_dispatch(x, expert_idx, gate_w, w_e1, w_e3, w_e2):
    m, d = x.shape
    n = m * TOP_K
    flat_e = expert_idx.reshape(n)
    order = jnp.argsort(flat_e)
    sorted_e = flat_e[order]
    counts = jnp.bincount(flat_e, length=N_EXPERTS)
    padded = (counts + MOE_ROWS - 1) // MOE_ROWS * MOE_ROWS
    pad_end = jnp.cumsum(padded)
    pad_start = pad_end - padded
    seg_start = jnp.cumsum(counts) - counts
    dest = pad_start[sorted_e] + jnp.arange(n) - seg_start[sorted_e]
    n_groups = (n + MOE_ROWS - 1) // MOE_ROWS + N_EXPERTS
    rows = n_groups * MOE_ROWS
    row_token = jnp.full((rows,), m, jnp.int32).at[dest].set((order // TOP_K).astype(jnp.int32))
    row_weight = jnp.zeros((rows,), gate_w.dtype).at[dest].set(gate_w.reshape(n)[order])
    group_expert = jnp.minimum(jnp.searchsorted(pad_end, jnp.arange(n_groups) * MOE_ROWS, side='right'), N_EXPERTS - 1)
    x_pad = jnp.concatenate([x, jnp.zeros((1, d), x.dtype)], axis=0)

    def run_group(args):
        tok, e, wt = args
        xb = x_pad[tok]
        return swiglu(xb, w_e1[e], w_e3[e], w_e2[e]) * wt[:, None].astype(x.dtype)

    y = lax.map(run_group, (row_token.reshape(n_groups, MOE_ROWS), group_expert, row_weight.reshape(n_groups, MOE_ROWS)))
    return jax.ops.segment_sum(y.reshape(rows, d), row_token, num_segments=m + 1)[:m]


def moe(h, w_router, b_router, w_e1, w_e3, w_e2, w_s1, w_s3, w_s2):
    n, t, d = h.shape
    x = h.reshape(n * t, d)
    m = x.shape[0]
    scores = jax.nn.sigmoid((x @ w_router).astype(jnp.float32))
    biased = scores + b_router.astype(jnp.float32)
    grp = biased.reshape(m, N_GROUPS, N_EXPERTS // N_GROUPS)
    grp_score = lax.top_k(grp, 2)[0].sum(axis=-1)
    _, top_groups = lax.top_k(grp_score, TOPK_GROUPS)
    group_mask = jnp.any(top_groups[..., None] == jnp.arange(N_GROUPS), axis=-2)
    expert_mask = jnp.repeat(group_mask, N_EXPERTS // N_GROUPS, axis=-1)
    _, idx = lax.top_k(jnp.where(expert_mask, biased, -jnp.inf), TOP_K)
    gw = jnp.take_along_axis(scores, idx, axis=-1)
    gw = gw / jnp.sum(gw, axis=-1, keepdims=True) * ROUTED_SCALE
    routed = moe_dispatch(x, idx, gw, w_e1, w_e3, w_e2)
    return (swiglu(x, w_s1, w_s3, w_s2) + routed).reshape(n, t, d)


def layer_forward(x, c, pos, attn_fn, gla_s0, gla_chunk, w_ada, b_ada, norm_mix_w, norm_ffn_w, w_in, w_gate_up,
                  b_gate, gla_norm_w, w_proj_a, w_proj_b, w_out, w_router, b_router, w_e1, w_e3, w_e2, w_s1, w_s3, w_s2):
    sh1, sc1, g1, sh2, sc2, g2 = adaln_mod(c, w_ada, b_ada)
    h = rmsnorm(x, norm_mix_w) * (1 + sc1) + sh1
    qa, ka, va, qb, kb, vb, log_a, gb, gate_a, gate_b = mixer_inputs(h, pos, w_in, w_gate_up, b_gate)
    oa = attn_fn(qa, ka, va)
    ob, s_new = gla_chunked(qb, kb, vb, log_a, gla_s0, gla_chunk)
    x = x + g1 * mixer_output(oa, ob, gb, gate_a, gate_b, gla_norm_w, w_proj_a, w_proj_b, w_out)
    h2 = rmsnorm(x, norm_ffn_w) * (1 + sc2) + sh2
    x = x + g2 * moe(h2, w_router, b_router, w_e1, w_e3, w_e2, w_s1, w_s3, w_s2)
    return x, ka, va, s_new


def setup_inputs(seed: int = 0) -> dict:
    key = jax.random.key(seed)
    ks = jax.random.split(key, 32)
    f32 = jnp.float32

    def nrm(k, shape, scale):
        return jax.random.normal(k, shape, f32) * scale

    n_pages = PAST_LEN // PAGE_SIZE
    n_pool = (DEC_BATCH * n_pages * 5) // 4
    perm = jax.random.permutation(ks[0], n_pool)
    page_table = perm[:DEC_BATCH * n_pages].reshape(DEC_BATCH, n_pages).astype(jnp.int32)
    return {
        'x_prompt': nrm(ks[1], (BATCH, SEQ, D_MODEL), 1.0),
        'x_sample': nrm(ks[2], (DEC_BATCH, DEC_SEQ, D_MODEL), 1.0),
        'cache_k': nrm(ks[3], (DEPTH, n_pool, PAGE_SIZE, A_HEADS, A_HEAD_DIM), 1.0),
        'cache_v': nrm(ks[4], (DEPTH, n_pool, PAGE_SIZE, A_HEADS, A_HEAD_DIM), 1.0),
        'state_gla': nrm(ks[5], (DEPTH, DEC_BATCH, B_HEADS, B_KEY_DIM, B_VAL_DIM), 3.0),
        'page_table': page_table,
        'c_prompt': nrm(ks[6], (BATCH, D_MODEL), 1.0),
        'c_sample': nrm(ks[7], (DEC_BATCH, D_MODEL), 1.0),
        'w_ada': nrm(ks[8], (DEPTH, D_MODEL, 6 * D_MODEL), 0.5 * D_MODEL ** -0.5),
        'b_ada': nrm(ks[9], (DEPTH, 6 * D_MODEL), 0.01),
        'norm_mix_w': 1.0 + nrm(ks[10], (DEPTH, D_MODEL), 0.01),
        'norm_ffn_w': 1.0 + nrm(ks[11], (DEPTH, D_MODEL), 0.01),
        'norm_final_w': 1.0 + nrm(ks[12], (D_MODEL,), 0.01),
        'w_in': nrm(ks[13], (DEPTH, D_MODEL, IN_WIDTH), D_MODEL ** -0.5),
        'w_gate_up': nrm(ks[14], (DEPTH, GATE_RANK, B_KEY_WIDTH), GATE_RANK ** -0.5),
        'b_gate': nrm(ks[15], (DEPTH, B_KEY_WIDTH), 0.1),
        'gla_norm_w': 1.0 + nrm(ks[16], (DEPTH, B_VAL_DIM), 0.01),
        'w_proj_a': nrm(ks[17], (DEPTH, A_WIDTH, D_MODEL), A_WIDTH ** -0.5),
        'w_proj_b': nrm(ks[18], (DEPTH, B_VAL_WIDTH, D_MODEL), B_VAL_WIDTH ** -0.5),
        'w_out': nrm(ks[19], (DEPTH, D_MODEL, D_MODEL), D_MODEL ** -0.5),
        'w_router': nrm(ks[20], (DEPTH, D_MODEL, N_EXPERTS), D_MODEL ** -0.5),
        'b_router': nrm(ks[21], (DEPTH, N_EXPERTS), 0.01),
        'w_e1': nrm(ks[22], (DEPTH, N_EXPERTS, D_MODEL, EXPERT_DIM), D_MODEL ** -0.5),
        'w_e3': nrm(ks[23], (DEPTH, N_EXPERTS, D_MODEL, EXPERT_DIM), D_MODEL ** -0.5),
        'w_e2': nrm(ks[24], (DEPTH, N_EXPERTS, EXPERT_DIM, D_MODEL), EXPERT_DIM ** -0.5),
        'w_s1': nrm(ks[25], (DEPTH, D_MODEL, SHARED_DIM), D_MODEL ** -0.5),
        'w_s3': nrm(ks[26], (DEPTH, D_MODEL, SHARED_DIM), D_MODEL ** -0.5),
        'w_s2': nrm(ks[27], (DEPTH, SHARED_DIM, D_MODEL), SHARED_DIM ** -0.5),
    }


def reference(x_prompt, x_sample, cache_k, cache_v, state_gla, page_table, c_prompt, c_sample,
              w_ada, b_ada, norm_mix_w, norm_ffn_w, norm_final_w, w_in, w_gate_up, b_gate, gla_norm_w,
              w_proj_a, w_proj_b, w_out, w_router, b_router, w_e1, w_e3, w_e2, w_s1, w_s3, w_s2):
    n_prompt, s_prompt = x_prompt.shape[:2]
    s_sample = x_sample.shape[1]
    pos_prompt = jnp.arange(s_prompt, dtype=jnp.int32)
    pos_sample = PAST_LEN + jnp.arange(s_sample, dtype=jnp.int32)
    xp, xs = x_prompt, x_sample
    kp_rows, vp_rows, sp_rows, ks_rows, vs_rows, ss_rows = [], [], [], [], [], []
    for l in range(DEPTH):
        lw = (w_ada[l], b_ada[l], norm_mix_w[l], norm_ffn_w[l], w_in[l], w_gate_up[l], b_gate[l], gla_norm_w[l],
              w_proj_a[l], w_proj_b[l], w_out[l], w_router[l], b_router[l], w_e1[l], w_e3[l], w_e2[l],
              w_s1[l], w_s3[l], w_s2[l])
        s0 = jnp.zeros((n_prompt, B_HEADS, B_KEY_DIM, B_VAL_DIM), state_gla.dtype)
        xp, k_l, v_l, s_l = layer_forward(xp, c_prompt, pos_prompt, moba_prompt, s0, GLA_CHUNK, *lw)
        kp_rows.append(k_l)
        vp_rows.append(v_l)
        sp_rows.append(s_l)
        sample_attn = functools.partial(moba_sample, cache_k=cache_k, cache_v=cache_v, page_table=page_table, layer=l)
        xs, k_l, v_l, s_l = layer_forward(xs, c_sample, pos_sample, sample_attn, state_gla[l], s_sample, *lw)
        ks_rows.append(k_l)
        vs_rows.append(v_l)
        ss_rows.append(s_l)
    y_prompt = rmsnorm(xp, norm_final_w)
    y_sample = rmsnorm(xs, norm_final_w)
    return (y_prompt, y_sample, jnp.stack(kp_rows), jnp.stack(vp_rows), jnp.stack(sp_rows),
            jnp.stack(ks_rows), jnp.stack(vs_rows), jnp.stack(ss_rows))
```

```python
import functools

import jax
import jax.numpy as jnp
from jax import lax
from jax.experimental import pallas as pl
from jax.experimental.pallas import tpu as pltpu

F32 = jnp.float32
BF16 = jnp.bfloat16
I32 = jnp.int32
HIGHEST = lax.Precision.HIGHEST

A_HEADS = 8
A_HEAD_DIM = 64
A_WIDTH = A_HEADS * A_HEAD_DIM
MOBA_BLOCK = 256
MOBA_TOPK = 3
ROPE_THETA = 10000.0
B_HEADS = 4
GATE_RANK = 16
GATE_NORMALIZER = 16.0
N_EXPERTS = 64
N_GROUPS = 8
GROUP_SIZE = N_EXPERTS // N_GROUPS
TOPK_GROUPS = 4
TOP_K = 8
ROUTED_SCALE = 2.5
EPS = 1e-6
NEG = -1e30
BELOW_NEG = -3e38

LANES = 128
SUBLANES = 8
VMEM_BYTES = 64 * 1024 * 1024

ROW_TILE = 256
GLA_CHUNK = 64
GLA_CHUNKS_PER_STEP = 4
MOE_GROUP_ROWS = 256


def _cparams(semantics, vmem_mb):
    return pltpu.CompilerParams(dimension_semantics=semantics, vmem_limit_bytes=vmem_mb * 1024 * 1024)


def _silu(x):
    return x * jax.nn.sigmoid(x)


def _dot(a, b):
    return jnp.dot(a, b, preferred_element_type=F32)


def _dot_nt(a, b, precision=None):
    return lax.dot_general(a, b, (((1,), (1,)), ((), ())), precision=precision, preferred_element_type=F32)


def _dot_tn(a, b):
    return lax.dot_general(a, b, (((0,), (0,)), ((), ())), preferred_element_type=F32)


def _ada_kernel(c_ref, w_ref, b_ref, o_ref):
    s = _silu(c_ref[...])
    o_ref[...] = jnp.dot(s, w_ref[...], precision=HIGHEST, preferred_element_type=F32) + b_ref[...]


def _ada_call(c_all, w_ada, b_ada):
    n, d = c_all.shape
    width = w_ada.shape[1]
    tn = width // 4
    return pl.pallas_call(
        _ada_kernel,
        grid=(width // tn,),
        in_specs=[
            pl.BlockSpec((n, d), lambda j: (0, 0)),
            pl.BlockSpec((d, tn), lambda j: (0, j)),
            pl.BlockSpec((1, tn), lambda j: (0, j)),
        ],
        out_specs=pl.BlockSpec((n, tn), lambda j: (0, j)),
        out_shape=jax.ShapeDtypeStruct((n, width), F32),
        compiler_params=_cparams(("arbitrary",), 40),
        name="adaln",
    )(c_all, w_ada, b_ada.reshape(1, width))


def _rope(a, cos, sin):
    lane = lax.broadcasted_iota(I32, cos.shape, 1)
    first_half = (lane % A_HEAD_DIM) < (A_HEAD_DIM // 2)
    outs = []
    for c in range(a.shape[1] // LANES):
        xc = a[:, c * LANES:(c + 1) * LANES]
        rot = jnp.where(first_half, -pltpu.roll(xc, LANES - A_HEAD_DIM // 2, 1), pltpu.roll(xc, A_HEAD_DIM // 2, 1))
        outs.append(xc * cos + rot * sin)
    return jnp.concatenate(outs, axis=1)


def _modulated_norm(x, norm_w, scale, shift):
    ms = jnp.mean(x * x, axis=-1, keepdims=True)
    y = x * lax.rsqrt(ms + EPS) * norm_w
    return y * (1.0 + scale) + shift


def _inproj_kernel(x_ref, sh_ref, sc_ref, nw_ref, cos_ref, sin_ref, w_ref, wgr_ref,
                   q_ref, k_ref, v_ref, qb_ref, kb_ref, vb_ref, gb_ref, ga_ref, gm_ref, gr_ref, *km_refs):
    g, r, d = x_ref.shape
    h = _modulated_norm(x_ref[...], nw_ref[...], sc_ref[...], sh_ref[...])
    hb = h.reshape(g * r, d).astype(BF16)
    cos = cos_ref[...]
    sin = sin_ref[...]
    col = 0

    def proj(width):
        nonlocal col
        out = _dot(hb, w_ref[:, col:col + width])
        col += width
        return out

    q_ref[...] = _rope(proj(A_WIDTH), cos, sin)
    k = _rope(proj(A_WIDTH), cos, sin)
    k_ref[...] = k
    if km_refs:
        nb = (g * r) // MOBA_BLOCK
        km_refs[0][...] = jnp.mean(k.reshape(nb, MOBA_BLOCK, A_WIDTH), axis=1, keepdims=True)
    v_ref[...] = proj(A_WIDTH)
    for ref in (qb_ref, kb_ref, vb_ref, gb_ref, ga_ref, gm_ref):
        ref[...] = proj(ref.shape[1])
    gr_ref[...] = _dot(hb, wgr_ref[...])


def _inproj_call(x3, mod3, norm_w, cos, sin, w_main, w_gr, *, groups_per_tile, rows_per_group, mod_of_tile,
                 cos_of_tile, with_kmean, name):
    n_groups, r, d = x3.shape
    assert r == rows_per_group
    m = n_groups * r
    tm = groups_per_tile * r
    n_tiles = n_groups // groups_per_tile
    d_model = d
    bk = w_main.shape[1] - 3 * A_WIDTH - 4 * d_model
    widths = [A_WIDTH, A_WIDTH, A_WIDTH, bk // 2, bk // 2, d_model, d_model, d_model, d_model, LANES]
    out_shape = [jax.ShapeDtypeStruct((m, w), F32) for w in widths]
    out_specs = [pl.BlockSpec((tm, w), lambda i: (i, 0)) for w in widths]
    if with_kmean:
        out_shape.append(jax.ShapeDtypeStruct((m // MOBA_BLOCK, 1, A_WIDTH), F32))
        out_specs.append(pl.BlockSpec((tm // MOBA_BLOCK, 1, A_WIDTH), lambda i: (i, 0, 0)))
    resident = dict(pipeline_mode=pl.Buffered(1))
    return pl.pallas_call(
        _inproj_kernel,
        grid=(n_tiles,),
        in_specs=[
            pl.BlockSpec((groups_per_tile, r, d), lambda i: (i, 0, 0)),
            pl.BlockSpec((groups_per_tile, 1, d), lambda i: (mod_of_tile(i), 0, 0)),
            pl.BlockSpec((groups_per_tile, 1, d), lambda i: (mod_of_tile(i), 0, 1)),
            pl.BlockSpec((1, d), lambda i: (0, 0), **resident),
            pl.BlockSpec((tm, LANES), lambda i: (cos_of_tile(i), 0)),
            pl.BlockSpec((tm, LANES), lambda i: (cos_of_tile(i), 0)),
            pl.BlockSpec(w_main.shape, lambda i: (0, 0), **resident),
            pl.BlockSpec(w_gr.shape, lambda i: (0, 0), **resident),
        ],
        out_specs=out_specs,
        out_shape=out_shape,
        compiler_params=_cparams(("parallel",), 48),
        name=name,
    )(x3, mod3, mod3, norm_w, cos, sin, w_main, w_gr)


def _select_kernel(q_ref, k_ref, v_ref, km_ref, qa_ref, ka_ref, va_ref, *, nbw):
    i = pl.program_id(1)
    q = q_ref[...]
    k = k_ref[...]
    v = v_ref[...]
    km = km_ref[:, 0, :]
    nblk = km.shape[0]
    rows = q.shape[0]
    col = lax.broadcasted_iota(I32, (rows, nblk), 1)
    colw = lax.broadcasted_iota(I32, (rows, nbw), 1)
    ones_col = (lax.broadcasted_iota(I32, (rows, A_HEAD_DIM), 1) == 0).astype(F32)
    own_onehot = jnp.where(colw == i, -NEG, 0.0)
    for h in range(A_HEADS):
        sl = slice(h * A_HEAD_DIM, (h + 1) * A_HEAD_DIM)
        qh = q[:, sl]
        s = _dot_nt(qh, km[:, sl], precision=HIGHEST)
        s = jnp.where(col < i, s, NEG)
        sel = col == i
        for _ in range(MOBA_TOPK):
            mx = jnp.max(s, axis=1, keepdims=True)
            first = jnp.min(jnp.where(s == mx, col, nblk), axis=1, keepdims=True)
            sel = sel | ((col == first) & (first < i))
            s = jnp.where(col == first, BELOW_NEG, s)
        selneg = jnp.where(sel, 0.0, -1.0)
        if nbw > nblk:
            selneg = jnp.concatenate([selneg, jnp.full((rows, nbw - nblk), -1.0, F32)], axis=1)
        qa_ref[0, h] = jnp.concatenate([qh * (A_HEAD_DIM ** -0.5), selneg], axis=1).astype(BF16)
        ka_ref[0, h] = jnp.concatenate([k[:, sl], own_onehot], axis=1).astype(BF16)
        va_ref[0, h] = jnp.concatenate([v[:, sl], ones_col], axis=1).astype(BF16)


def _select_call(q, k, v, kmean, batch, seq):
    nblk = seq // MOBA_BLOCK
    nbw = -(-nblk // A_HEAD_DIM) * A_HEAD_DIM
    aw = A_HEAD_DIM + nbw
    row_spec = pl.BlockSpec((MOBA_BLOCK, A_WIDTH), lambda b, i: (b * nblk + i, 0))
    head_spec = lambda w: pl.BlockSpec((1, A_HEADS, MOBA_BLOCK, w), lambda b, i: (b, 0, i, 0))
    return pl.pallas_call(
        functools.partial(_select_kernel, nbw=nbw),
        grid=(batch, nblk),
        in_specs=[row_spec, row_spec, row_spec, pl.BlockSpec((nblk, 1, A_WIDTH), lambda b, i: (b, 0, 0))],
        out_specs=[head_spec(aw), head_spec(aw), head_spec(2 * A_HEAD_DIM)],
        out_shape=[
            jax.ShapeDtypeStruct((batch, A_HEADS, seq, aw), BF16),
            jax.ShapeDtypeStruct((batch, A_HEADS, seq, aw), BF16),
            jax.ShapeDtypeStruct((batch, A_HEADS, seq, 2 * A_HEAD_DIM), BF16),
        ],
        compiler_params=_cparams(("parallel", "parallel"), 32),
        name="moba_select",
    )(q, k, v, kmean)


def _flash_step(s, vblk, m_ref, acc_ref):
    m_prev = m_ref[...]
    m_new = jnp.maximum(m_prev, jnp.max(s, axis=1, keepdims=True))
    alpha = jnp.exp(m_prev - m_new)
    p = jnp.exp(s - jnp.concatenate([m_new] * (s.shape[1] // LANES), axis=1))
    acc_ref[...] = alpha * acc_ref[...] + _dot(p.astype(BF16), vblk)
    m_ref[...] = m_new


def _moba_attn_kernel(qa_ref, ka_ref, va_ref, o_ref, m_ref, acc_ref):
    i = pl.program_id(2)
    blk = MOBA_BLOCK
    qa = qa_ref[0, 0]
    m_ref[...] = jnp.full(m_ref.shape, BELOW_NEG, F32)
    acc_ref[...] = jnp.zeros(acc_ref.shape, F32)

    def past_block(j, carry):
        r0 = pl.multiple_of(j * blk, blk)
        s = _dot_nt(qa, ka_ref[0, 0, pl.ds(r0, blk), :])
        _flash_step(s, va_ref[0, 0, pl.ds(r0, blk), :], m_ref, acc_ref)
        return carry

    lax.fori_loop(0, i, past_block, 0)
    r0 = pl.multiple_of(i * blk, blk)
    s = _dot_nt(qa, ka_ref[0, 0, pl.ds(r0, blk), :])
    row = lax.broadcasted_iota(I32, s.shape, 0)
    colk = lax.broadcasted_iota(I32, s.shape, 1)
    s = jnp.where(colk <= row, s, NEG)
    _flash_step(s, va_ref[0, 0, pl.ds(r0, blk), :], m_ref, acc_ref)
    acc = acc_ref[...]
    o_ref[0, 0] = acc[:, :A_HEAD_DIM] / acc[:, A_HEAD_DIM:A_HEAD_DIM + 1]


def _moba_attn_call(qaug, kaug, vaug):
    batch, heads, seq, aw = qaug.shape
    nblk = seq // MOBA_BLOCK
    return pl.pallas_call(
        _moba_attn_kernel,
        grid=(batch, heads, nblk),
        in_specs=[
            pl.BlockSpec((1, 1, MOBA_BLOCK, aw), lambda b, h, i: (b, h, i, 0)),
            pl.BlockSpec((1, 1, seq, aw), lambda b, h, i: (b, h, 0, 0)),
            pl.BlockSpec((1, 1, seq, 2 * A_HEAD_DIM), lambda b, h, i: (b, h, 0, 0)),
        ],
        out_specs=pl.BlockSpec((1, 1, MOBA_BLOCK, A_HEAD_DIM), lambda b, h, i: (b, h, i, 0)),
        out_shape=jax.ShapeDtypeStruct((batch, heads, seq, A_HEAD_DIM), F32),
        scratch_shapes=[pltpu.VMEM((MOBA_BLOCK, LANES), F32), pltpu.VMEM((MOBA_BLOCK, LANES), F32)],
        compiler_params=_cparams(("parallel", "parallel", "arbitrary"), 40),
        name="moba_attention",
    )(qaug, kaug, vaug)


def _gla_kernel(*refs, chunk, n_chunks, has_s0):
    if has_s0:
        q_ref, k_ref, v_ref, gr_ref, wg_ref, bg_ref, s0_ref, o_ref, sout_ref, st_ref = refs
    else:
        q_ref, k_ref, v_ref, gr_ref, wg_ref, bg_ref, o_ref, sout_ref, st_ref = refs
    step = pl.program_id(2)
    dk = q_ref.shape[1]

    @pl.when(step == 0)
    def _():
        if has_s0:
            st_ref[...] = s0_ref[0, 0].T
        else:
            st_ref[...] = jnp.zeros(st_ref.shape, F32)

    rows = lax.broadcasted_iota(I32, (chunk, chunk), 0)
    cols = lax.broadcasted_iota(I32, (chunk, chunk), 1)
    causal = cols <= rows
    tri = causal.astype(F32)
    wg = wg_ref[0]
    bg = bg_ref[0]
    for c in range(n_chunks):
        sl = slice(c * chunk, (c + 1) * chunk)
        q = q_ref[sl, :] * (dk ** -0.5)
        k = k_ref[sl, :]
        v = v_ref[sl, :].astype(BF16)
        z = jnp.dot(gr_ref[sl, :], wg, precision=HIGHEST, preferred_element_type=F32) + bg
        log_a = -(jnp.maximum(-z, 0.0) + jnp.log1p(jnp.exp(-jnp.abs(z)))) * (1.0 / GATE_NORMALIZER)
        b = jnp.dot(tri, log_a, precision=HIGHEST, preferred_element_type=F32)
        b_last = b[chunk - 1:chunk, :]
        qe = (q * jnp.exp(b)).astype(BF16)
        ke = (k * jnp.exp(-b)).astype(BF16)
        kd = (k * jnp.exp(b_last - b)).astype(BF16)
        st = st_ref[...]
        scores = jnp.where(causal, _dot_nt(qe, ke), 0.0)
        o_ref[sl, :] = _dot(scores.astype(BF16), v) + _dot_nt(qe, st.astype(BF16))
        st_ref[...] = st * jnp.exp(b_last) + _dot_tn(v, kd)

    @pl.when(step == pl.num_programs(2) - 1)
    def _():
        sout_ref[0, 0] = st_ref[...].T


def _gla_call(qb, kb, vb, gr, wg, bg, s0, *, n_seq, seq_len, chunk, chunks_per_step, name):
    heads = B_HEADS
    dk = qb.shape[1] // heads
    dv = vb.shape[1] // heads
    rb = chunk * chunks_per_step
    steps = seq_len // rb
    row = lambda n, h, s: n * steps + s
    in_specs = [
        pl.BlockSpec((rb, dk), lambda n, h, s: (row(n, h, s), h)),
        pl.BlockSpec((rb, dk), lambda n, h, s: (row(n, h, s), h)),
        pl.BlockSpec((rb, dv), lambda n, h, s: (row(n, h, s), h)),
        pl.BlockSpec((rb, LANES), lambda n, h, s: (row(n, h, s), 0)),
        pl.BlockSpec((1, LANES, dk), lambda n, h, s: (h, 0, 0)),
        pl.BlockSpec((1, 1, dk), lambda n, h, s: (h, 0, 0)),
    ]
    args = [qb, kb, vb, gr, wg, bg]
    if s0 is not None:
        in_specs.append(pl.BlockSpec((1, 1, dk, dv), lambda n, h, s: (n, h, 0, 0)))
        args.append(s0)
    return pl.pallas_call(
        functools.partial(_gla_kernel, chunk=chunk, n_chunks=chunks_per_step, has_s0=s0 is not None),
        grid=(n_seq, heads, steps),
        in_specs=in_specs,
        out_specs=[
            pl.BlockSpec((rb, dv), lambda n, h, s: (row(n, h, s), h)),
            pl.BlockSpec((1, 1, dk, dv), lambda n, h, s: (n, h, 0, 0)),
        ],
        out_shape=[
            jax.ShapeDtypeStruct((n_seq * seq_len, heads * dv), F32),
            jax.ShapeDtypeStruct((n_seq, heads, dk, dv), F32),
        ],
        scratch_shapes=[pltpu.VMEM((dv, dk), F32)],
        compiler_params=_cparams(("parallel", "parallel", "arbitrary"), 32),
        name=name,
    )(*args)


def _mixer_kernel(x_ref, oa_ref, ob_ref, gb_ref, ga_ref, gm_ref, g1_ref, sh2_ref, sc2_ref, nw2_ref, gnw_ref,
                  wpa_ref, wpb_ref, wout_ref, wr_ref, x1_ref, h2_ref, lg_ref):
    g, r, d = x_ref.shape
    tm = g * r
    dv = gnw_ref.shape[1]
    ob = ob_ref[...]
    gb = gb_ref[...]
    gnw = gnw_ref[...]
    parts = []
    for h in range(ob.shape[1] // dv):
        oh = ob[:, h * dv:(h + 1) * dv]
        ms = jnp.mean(oh * oh, axis=-1, keepdims=True)
        parts.append((oh * lax.rsqrt(ms + EPS) * gnw * _silu(gb[:, h * dv:(h + 1) * dv])).astype(BF16))
    pb = _dot(jnp.concatenate(parts, axis=1), wpb_ref[...])
    pa = jnp.zeros((tm, d), F32)
    for h in range(A_HEADS):
        pa = pa + _dot(oa_ref[0, h].astype(BF16), wpa_ref[h])
    merged = jax.nn.sigmoid(ga_ref[...]) * pa + jax.nn.sigmoid(gm_ref[...]) * pb
    u = _dot(merged.astype(BF16), wout_ref[...])
    x1 = x_ref[...] + g1_ref[...] * u.reshape(g, r, d)
    x1_ref[...] = x1
    h2 = _modulated_norm(x1, nw2_ref[...], sc2_ref[...], sh2_ref[...]).reshape(tm, d)
    h2_ref[...] = h2
    lg_ref[...] = _dot_nt(wr_ref[...], h2, precision=HIGHEST)


def _mixer_call(x3, oa4, ob, gb, ga, gm, mod3, nw2, gnw, wpa, wpb, wout, wr_t, *, groups_per_tile, mod_of_tile,
                oa_of_tile, name):
    n_groups, r, d = x3.shape
    m = n_groups * r
    tm = groups_per_tile * r
    n_tiles = n_groups // groups_per_tile
    row = lambda w: pl.BlockSpec((tm, w), lambda i: (i, 0))
    modspec = lambda k: pl.BlockSpec((groups_per_tile, 1, d), lambda i: (mod_of_tile(i), 0, k))
    const = lambda a: pl.BlockSpec(a.shape, lambda i: (0,) * a.ndim, pipeline_mode=pl.Buffered(1))
    return pl.pallas_call(
        _mixer_kernel,
        grid=(n_tiles,),
        in_specs=[
            pl.BlockSpec((groups_per_tile, r, d), lambda i: (i, 0, 0)),
            pl.BlockSpec((1, A_HEADS, tm, A_HEAD_DIM), lambda i: oa_of_tile(i)),
            row(ob.shape[1]), row(gb.shape[1]), row(d), row(d),
            modspec(2), modspec(3), modspec(4),
            const(nw2), const(gnw), const(wpa), const(wpb), const(wout), const(wr_t),
        ],
        out_specs=[
            pl.BlockSpec((groups_per_tile, r, d), lambda i: (i, 0, 0)),
            pl.BlockSpec((tm, d), lambda i: (i, 0)),
            pl.BlockSpec((N_EXPERTS, tm), lambda i: (0, i)),
        ],
        out_shape=[
            jax.ShapeDtypeStruct((n_groups, r, d), F32),
            jax.ShapeDtypeStruct((m, d), F32),
            jax.ShapeDtypeStruct((N_EXPERTS, m), F32),
        ],
        compiler_params=_cparams(("parallel",), 48),
        name=name,
    )(x3, oa4, ob, gb, ga, gm, mod3, mod3, mod3, nw2, gnw, wpa, wpb, wout, wr_t)


def _first_argmax(vals, iota, axis, size):
    mx = jnp.max(vals, axis=axis, keepdims=True)
    return jnp.min(jnp.where(vals == mx, iota, size), axis=axis, keepdims=True)


def _route_kernel(lg_ref, br_ref, idx_ref, gw_ref):
    logits = lg_ref[...]
    n_e, tm = logits.shape
    scores = jax.nn.sigmoid(logits)
    biased = scores + jnp.concatenate([br_ref[...]] * (tm // LANES), axis=1)
    g3 = biased.reshape(N_GROUPS, GROUP_SIZE, tm)
    sub = lax.broadcasted_iota(I32, g3.shape, 1)
    m1 = jnp.max(g3, axis=1, keepdims=True)
    f1 = jnp.min(jnp.where(g3 == m1, sub, GROUP_SIZE), axis=1, keepdims=True)
    m2 = jnp.max(jnp.where(sub == f1, -jnp.inf, g3), axis=1, keepdims=True)
    gs = (m1 + m2).reshape(N_GROUPS, tm)
    giota = lax.broadcasted_iota(I32, gs.shape, 0)
    gmask = jnp.zeros(gs.shape, jnp.bool_)
    for _ in range(TOPK_GROUPS):
        first = _first_argmax(gs, giota, 0, N_GROUPS)
        hit = giota == first
        gmask = gmask | hit
        gs = jnp.where(hit, -jnp.inf, gs)
    emask = jnp.broadcast_to(gmask.reshape(N_GROUPS, 1, tm), (N_GROUPS, GROUP_SIZE, tm)).reshape(n_e, tm)
    masked = jnp.where(emask, biased, -jnp.inf)
    eiota = lax.broadcasted_iota(I32, masked.shape, 0)
    idx_rows, w_rows = [], []
    for _ in range(TOP_K):
        first = _first_argmax(masked, eiota, 0, n_e)
        hit = eiota == first
        idx_rows.append(first)
        w_rows.append(jnp.sum(jnp.where(hit, scores, 0.0), axis=0, keepdims=True))
        masked = jnp.where(hit, -jnp.inf, masked)
    w = jnp.concatenate(w_rows, axis=0)
    idx_ref[...] = jnp.concatenate(idx_rows, axis=0)
    gw_ref[...] = w / jnp.sum(w, axis=0, keepdims=True) * ROUTED_SCALE


def _route_call(logits_t, b_router_lanes):
    n_e, m = logits_t.shape
    tm = 512 if m % 512 == 0 else ROW_TILE
    return pl.pallas_call(
        _route_kernel,
        grid=(m // tm,),
        in_specs=[pl.BlockSpec((n_e, tm), lambda i: (0, i)), pl.BlockSpec((n_e, LANES), lambda i: (0, 0))],
        out_specs=[pl.BlockSpec((TOP_K, tm), lambda i: (0, i)), pl.BlockSpec((TOP_K, tm), lambda i: (0, i))],
        out_shape=[jax.ShapeDtypeStruct((TOP_K, m), I32), jax.ShapeDtypeStruct((TOP_K, m), F32)],
        compiler_params=_cparams(("parallel",), 32),
        name="moe_route",
    )(logits_t, b_router_lanes)


SMEM_INDEX_BLOCK = 1024


def _start_row_gather(idx_ref, idx_base, n_rows, src_hbm, dst, sem):
    def body(r, carry):
        t = idx_ref[idx_base + r]
        pltpu.make_async_copy(src_hbm.at[pl.ds(t, 1), :], dst.at[pl.ds(r, 1), :], sem).start()
        return carry

    lax.fori_loop(0, n_rows, body, 0)


def _wait_row_gather(n_rows, src_hbm, dst, sem):
    pltpu.make_async_copy(src_hbm.at[pl.ds(0, n_rows), :], dst, sem).wait()


def _expert_kernel(ge_ref, tok_ref, tok_next_ref, wt_ref, x_hbm, w1_ref, w3_ref, w2_ref, y_ref, xbuf, sem):
    g = pl.program_id(0)
    n_g = pl.num_programs(0)
    rows = y_ref.shape[0]
    per_block = SMEM_INDEX_BLOCK // rows
    slot = g % 2

    @pl.when(g == 0)
    def _():
        _start_row_gather(tok_ref, 0, rows, x_hbm, xbuf.at[0], sem.at[0])

    @pl.when(g + 1 < n_g)
    def _():
        _start_row_gather(tok_next_ref, ((g + 1) % per_block) * rows, rows, x_hbm, xbuf.at[1 - slot], sem.at[1 - slot])

    _wait_row_gather(rows, x_hbm, xbuf.at[slot], sem.at[slot])
    xb = xbuf[slot].astype(BF16)
    hidden = _silu(_dot(xb, w1_ref[0])) * _dot(xb, w3_ref[0])
    y = _dot(hidden.astype(BF16), w2_ref[0])
    wt_col = jnp.broadcast_to(wt_ref[0], (LANES, rows)).T
    y_ref[...] = y * jnp.concatenate([wt_col] * (y.shape[1] // LANES), axis=1)


def _expert_call(group_expert, row_token, row_weight, h2, w1, w3, w2):
    n_groups = group_expert.shape[0]
    rows = MOE_GROUP_ROWS
    m, d = h2.shape
    per_block = SMEM_INDEX_BLOCK // rows
    n_blocks = n_groups // per_block
    hidden = w1.shape[2]
    grid_spec = pltpu.PrefetchScalarGridSpec(
        num_scalar_prefetch=1,
        grid=(n_groups,),
        in_specs=[
            pl.BlockSpec((SMEM_INDEX_BLOCK,), lambda g, ge: (g // per_block,), memory_space=pltpu.SMEM),
            pl.BlockSpec((SMEM_INDEX_BLOCK,), lambda g, ge: (jnp.minimum(g + 1, n_groups - 1) // per_block,),
                         memory_space=pltpu.SMEM),
            pl.BlockSpec((1, 1, rows), lambda g, ge: (g, 0, 0)),
            pl.BlockSpec(memory_space=pl.ANY),
            pl.BlockSpec((1, d, hidden), lambda g, ge: (ge[g], 0, 0)),
            pl.BlockSpec((1, d, hidden), lambda g, ge: (ge[g], 0, 0)),
            pl.BlockSpec((1, hidden, d), lambda g, ge: (ge[g], 0, 0)),
        ],
        out_specs=pl.BlockSpec((rows, d), lambda g, ge: (g, 0)),
        scratch_shapes=[pltpu.VMEM((2, rows, d), F32), pltpu.SemaphoreType.DMA((2,))],
    )
    del n_blocks
    return pl.pallas_call(
        _expert_kernel,
        grid_spec=grid_spec,
        out_shape=jax.ShapeDtypeStruct((n_groups * rows, d), F32),
        compiler_params=_cparams(("arbitrary",), 40),
        name="moe_experts",
    )(group_expert, row_token, row_token, row_weight.reshape(n_groups, 1, rows), h2, w1, w3, w2)


def _final_kernel(pos_ref, pos_next_ref, x1_ref, h2_ref, g2_ref, nwf_ref, ws1_ref, ws3_ref, ws2_ref, ys_hbm,
                  y_ref, buf, sem):
    i = pl.program_id(0)
    n_i = pl.num_programs(0)
    g, r, d = x1_ref.shape
    tm = g * r
    n_rows = TOP_K * tm
    slot = i % 2

    @pl.when(i == 0)
    def _():
        _start_row_gather(pos_ref, 0, n_rows, ys_hbm, buf.at[0], sem.at[0])

    @pl.when(i + 1 < n_i)
    def _():
        _start_row_gather(pos_next_ref, 0, n_rows, ys_hbm, buf.at[1 - slot], sem.at[1 - slot])

    hb = h2_ref[...].astype(BF16)
    shared = _dot((_silu(_dot(hb, ws1_ref[...])) * _dot(hb, ws3_ref[...])).astype(BF16), ws2_ref[...])
    _wait_row_gather(n_rows, ys_hbm, buf.at[slot], sem.at[slot])
    routed = buf[slot, 0:tm, :]
    for k in range(1, TOP_K):
        routed = routed + buf[slot, k * tm:(k + 1) * tm, :]
    x2 = x1_ref[...] + g2_ref[...] * (shared + routed).reshape(g, r, d)
    ms = jnp.mean(x2 * x2, axis=-1, keepdims=True)
    y_ref[...] = x2 * lax.rsqrt(ms + EPS) * nwf_ref[...]


def _final_call(pos_tiles, x3, h2, mod3, nwf, ws1, ws3, ws2, ysorted, *, groups_per_tile, mod_of_tile, name):
    n_groups, r, d = x3.shape
    tm = groups_per_tile * r
    n_tiles = n_groups // groups_per_tile
    n_rows = TOP_K * tm
    const = lambda a: pl.BlockSpec(a.shape, lambda i: (0,) * a.ndim, pipeline_mode=pl.Buffered(1))
    return pl.pallas_call(
        _final_kernel,
        grid=(n_tiles,),
        in_specs=[
            pl.BlockSpec((n_rows,), lambda i: (i,), memory_space=pltpu.SMEM),
            pl.BlockSpec((n_rows,), lambda i: (jnp.minimum(i + 1, n_tiles - 1),), memory_space=pltpu.SMEM),
            pl.BlockSpec((groups_per_tile, r, d), lambda i: (i, 0, 0)),
            pl.BlockSpec((tm, d), lambda i: (i, 0)),
            pl.BlockSpec((groups_per_tile, 1, d), lambda i: (mod_of_tile(i), 0, 5)),
            const(nwf), const(ws1), const(ws3), const(ws2),
            pl.BlockSpec(memory_space=pl.ANY),
        ],
        out_specs=pl.BlockSpec((groups_per_tile, r, d), lambda i: (i, 0, 0)),
        out_shape=jax.ShapeDtypeStruct((n_groups, r, d), F32),
        scratch_shapes=[pltpu.VMEM((2, n_rows, d), F32), pltpu.SemaphoreType.DMA((2,))],
        compiler_params=_cparams(("arbitrary",), 56),
        name=name,
    )(pos_tiles, pos_tiles, x3, h2, mod3, nwf, ws1, ws3, ws2, ysorted)


def _paged_attn_kernel(pt_ref, ck_hbm, cv_hbm, q_ref, kn_ref, vn_ref, o_ref,
                       cbuf, sem, qbd_ref, qbdf_ref, s_ref, p_ref, km_ref, l_ref, acc_ref,
                       *, pages_per_seq, n_sel):
    n = pl.program_id(0)
    u = pl.program_id(1)
    n_seq = pl.num_programs(0)
    n_u = pl.num_programs(1)
    n_kc = n_u // 2
    _, pages_step, page, width = cbuf.shape
    rows_step = pages_step * page
    blocks_step = rows_step // MOBA_BLOCK
    t_new = q_ref.shape[0]
    n_blocks = km_ref.shape[0]
    slot = (n * n_u + u) % 2

    def start_chunk(nn, uu, s):
        def copy_pages(src_hbm, cc):
            for p in range(pages_step):
                page_id = pt_ref[nn * pages_per_seq + cc * pages_step + p]
                pltpu.make_async_copy(src_hbm.at[page_id], cbuf.at[s, p], sem.at[s]).start()

        @pl.when(uu < n_kc)
        def _():
            copy_pages(ck_hbm, uu)

        @pl.when(uu >= n_kc)
        def _():
            copy_pages(cv_hbm, uu - n_kc)

    @pl.when((n == 0) & (u == 0))
    def _():
        start_chunk(n, u, slot)

    last_u = u == n_u - 1

    @pl.when(jnp.logical_not(last_u & (n == n_seq - 1)))
    def _():
        start_chunk(jnp.where(last_u, n + 1, n), jnp.where(last_u, 0, u + 1), 1 - slot)

    @pl.when(u == 0)
    def _():
        q = q_ref[...] * (A_HEAD_DIM ** -0.5)
        t_sel = (lax.broadcasted_iota(I32, (t_new, LANES), 1) % t_new
                 == lax.broadcasted_iota(I32, (t_new, LANES), 0)).astype(F32)
        spread = lax.dot_general(q, t_sel, (((0,), (0,)), ((), ())), precision=HIGHEST, preferred_element_type=F32)
        r_i = lax.broadcasted_iota(I32, (width, LANES), 0)
        c_i = lax.broadcasted_iota(I32, (width, LANES), 1)
        qbd = jnp.where((r_i // A_HEAD_DIM == c_i // t_new) & (c_i < A_HEADS * t_new), spread, 0.0)
        qbdf_ref[...] = qbd
        qbd_ref[...] = qbd.astype(BF16)

    pltpu.make_async_copy(ck_hbm.at[pl.ds(0, pages_step)], cbuf.at[slot], sem.at[slot]).wait()
    chunk = cbuf[slot].reshape(rows_step, width)

    @pl.when(u < n_kc)
    def _():
        b0 = pl.multiple_of(u * blocks_step, blocks_step)
        km_ref[pl.ds(b0, blocks_step), :] = jnp.mean(chunk.reshape(blocks_step, MOBA_BLOCK, width), axis=1)
        r0 = pl.multiple_of(u * rows_step, rows_step)
        s_ref[pl.ds(r0, rows_step), :] = _dot(chunk.astype(BF16), qbd_ref[...])

    @pl.when(u == n_kc - 1)
    def _():
        sc = jnp.dot(km_ref[...], qbdf_ref[...], precision=HIGHEST, preferred_element_type=F32)
        biota = lax.broadcasted_iota(I32, sc.shape, 0)
        sel = jnp.zeros(sc.shape, jnp.bool_)
        for _ in range(n_sel):
            first = _first_argmax(sc, biota, 0, n_blocks)
            hit = biota == first
            sel = sel | hit
            sc = jnp.where(hit, BELOW_NEG, sc)
        bias = jnp.where(sel, 0.0, NEG)
        s_new = _dot(kn_ref[...].astype(BF16), qbd_ref[...])
        r_i = lax.broadcasted_iota(I32, s_new.shape, 0)
        c_i = lax.broadcasted_iota(I32, s_new.shape, 1)
        s_new = jnp.where(r_i <= c_i % t_new, s_new, NEG)
        m = jnp.max(s_new, axis=0, keepdims=True)
        for j in range(n_blocks):
            sj = s_ref[j * MOBA_BLOCK:(j + 1) * MOBA_BLOCK, :] + bias[j:j + 1, :]
            m = jnp.maximum(m, jnp.max(sj, axis=0, keepdims=True))
        p_new = jnp.exp(s_new - m)
        l = jnp.sum(p_new, axis=0, keepdims=True)
        for j in range(n_blocks):
            pj = jnp.exp(s_ref[j * MOBA_BLOCK:(j + 1) * MOBA_BLOCK, :] + bias[j:j + 1, :] - m)
            l = l + jnp.sum(pj, axis=0, keepdims=True)
            p_ref[j * MOBA_BLOCK:(j + 1) * MOBA_BLOCK, :] = pj.astype(BF16)
        l_ref[...] = jnp.broadcast_to(l, l_ref.shape)
        acc_ref[...] = _dot_tn(vn_ref[...].astype(BF16), p_new.astype(BF16))

    @pl.when(u >= n_kc)
    def _():
        r0 = pl.multiple_of((u - n_kc) * rows_step, rows_step)
        acc_ref[...] += _dot_tn(chunk.astype(BF16), p_ref[pl.ds(r0, rows_step), :])

    @pl.when(last_u)
    def _():
        o_ref[0] = acc_ref[...] / l_ref[0:1, :]


def _paged_attn_call(page_table_flat, cache_k3, cache_v3, q, k_new, v_new, n_seq, pages_per_seq, t_new):
    _, page, width = cache_k3.shape
    past = pages_per_seq * page
    n_blocks = past // MOBA_BLOCK
    pages_step = 16
    while pages_per_seq % pages_step:
        pages_step //= 2
    assert (pages_step * page) % MOBA_BLOCK == 0 and A_HEADS * t_new <= LANES
    n_kc = pages_per_seq // pages_step
    n_sel = min(MOBA_TOPK, n_blocks)
    new_spec = pl.BlockSpec((t_new, width), lambda n, u, pt: (n, 0))
    grid_spec = pltpu.PrefetchScalarGridSpec(
        num_scalar_prefetch=1,
        grid=(n_seq, 2 * n_kc),
        in_specs=[pl.BlockSpec(memory_space=pl.ANY), pl.BlockSpec(memory_space=pl.ANY), new_spec, new_spec, new_spec],
        out_specs=pl.BlockSpec((1, width, LANES), lambda n, u, pt: (n, 0, 0)),
        scratch_shapes=[
            pltpu.VMEM((2, pages_step, page, width), F32),
            pltpu.SemaphoreType.DMA((2,)),
            pltpu.VMEM((width, LANES), BF16),
            pltpu.VMEM((width, LANES), F32),
            pltpu.VMEM((past, LANES), F32),
            pltpu.VMEM((past, LANES), BF16),
            pltpu.VMEM((n_blocks, width), F32),
            pltpu.VMEM((SUBLANES, LANES), F32),
            pltpu.VMEM((width, LANES), F32),
        ],
    )
    return pl.pallas_call(
        functools.partial(_paged_attn_kernel, pages_per_seq=pages_per_seq, n_sel=n_sel),
        grid_spec=grid_spec,
        out_shape=jax.ShapeDtypeStruct((n_seq, width, LANES), F32),
        compiler_params=_cparams(("arbitrary", "arbitrary"), 40),
        name="sample_attention",
    )(page_table_flat, cache_k3, cache_v3, q, k_new, v_new)


def _rope_tables(pos):
    inv_freq = ROPE_THETA ** (-jnp.arange(0, A_HEAD_DIM, 2, dtype=F32) / A_HEAD_DIM)
    ang = pos.astype(F32)[:, None] * inv_freq[None, :]
    cos = jnp.concatenate([jnp.cos(ang)] * (2 * LANES // A_HEAD_DIM), axis=-1)
    sin = jnp.concatenate([jnp.sin(ang)] * (2 * LANES // A_HEAD_DIM), axis=-1)
    return cos, sin


def _dispatch_tables(idx, gw):
    k, m = idx.shape
    n = k * m
    rows_g = MOE_GROUP_ROWS
    per_block = SMEM_INDEX_BLOCK // rows_g
    flat_e = idx.reshape(n)
    order = jnp.argsort(flat_e, stable=True).astype(I32)
    sorted_e = flat_e[order]
    counts = jnp.zeros((N_EXPERTS,), I32).at[flat_e].add(1)
    padded = (counts + rows_g - 1) // rows_g * rows_g
    pad_end = jnp.cumsum(padded)
    pad_start = pad_end - padded
    seg_start = jnp.cumsum(counts) - counts
    dest = pad_start[sorted_e] + jnp.arange(n, dtype=I32) - seg_start[sorted_e]
    n_groups = -(-(-(-n // rows_g) + N_EXPERTS) // per_block) * per_block
    rows = n_groups * rows_g
    row_token = jnp.zeros((rows,), I32).at[dest].set(order % m)
    row_weight = jnp.zeros((rows,), F32).at[dest].set(gw.reshape(n)[order])
    group_expert = jnp.minimum(
        jnp.searchsorted(pad_end, jnp.arange(n_groups, dtype=I32) * rows_g, side="right"), N_EXPERTS - 1).astype(I32)
    pos = jnp.zeros((n,), I32).at[order].set(dest).reshape(k, m)
    return group_expert, row_token, row_weight, pos


def kernel(x_prompt, x_sample, cache_k, cache_v, state_gla, page_table, c_prompt, c_sample, w_ada, b_ada,
           norm_mix_w, norm_ffn_w, norm_final_w, w_in, w_gate_up, b_gate, gla_norm_w, w_proj_a, w_proj_b, w_out,
           w_router, b_router, w_e1, w_e3, w_e2, w_s1, w_s3, w_s2):
    batch, seq, d = x_prompt.shape
    n_dec, t_new, _ = x_sample.shape
    depth = w_ada.shape[0]
    n_pool, page = cache_k.shape[1], cache_k.shape[2]
    pages_per_seq = page_table.shape[1]
    past_len = pages_per_seq * page
    assert depth == 1 and past_len % MOBA_BLOCK == 0 and MOBA_BLOCK % page == 0
    assert seq % ROW_TILE == 0 and ROW_TILE % MOBA_BLOCK == 0 and ROW_TILE % t_new == 0 and t_new == SUBLANES
    m_p, m_s = batch * seq, n_dec * t_new
    bkw = w_gate_up.shape[2]
    dk = bkw // B_HEADS
    dv = gla_norm_w.shape[1]
    bvw = B_HEADS * dv

    n_c = batch + n_dec
    n_c_pad = -(-n_c // SUBLANES) * SUBLANES
    c_all = jnp.concatenate([c_prompt, c_sample, jnp.zeros((n_c_pad - n_c, d), F32)], axis=0)
    mod = _ada_call(c_all, w_ada[0], b_ada[0])
    mod_p = mod[:batch].reshape(batch, 1, 6 * d)
    mod_s = mod[batch:n_c].reshape(n_dec, 1, 6 * d)

    sizes = (A_WIDTH, A_WIDTH, A_WIDTH, bkw, bkw, bvw, bvw, GATE_RANK, d, d)
    offs = [0]
    for s in sizes:
        offs.append(offs[-1] + s)
    w_in0 = w_in[0]
    w_main = jnp.concatenate([w_in0[:, :offs[7]], w_in0[:, offs[8]:]], axis=1).astype(BF16)
    w_gr = jnp.pad(w_in0[:, offs[7]:offs[8]], ((0, 0), (0, LANES - GATE_RANK))).astype(BF16)
    wg = jnp.pad(w_gate_up[0].reshape(GATE_RANK, B_HEADS, dk).transpose(1, 0, 2),
                 ((0, 0), (0, LANES - GATE_RANK), (0, 0)))
    bg = b_gate[0].reshape(B_HEADS, 1, dk)
    nw1 = norm_mix_w[0].reshape(1, d)
    nw2 = norm_ffn_w[0].reshape(1, d)
    nwf = norm_final_w.reshape(1, d)
    gnw = gla_norm_w[0].reshape(1, dv)
    wpa = w_proj_a[0].astype(BF16).reshape(A_HEADS, A_HEAD_DIM, d)
    wpb = w_proj_b[0].astype(BF16)
    wout = w_out[0].astype(BF16)
    wr_t = w_router[0].T
    br_lanes = jnp.broadcast_to(b_router[0][:, None], (N_EXPERTS, LANES))
    we1, we3, we2 = w_e1[0].astype(BF16), w_e3[0].astype(BF16), w_e2[0].astype(BF16)
    ws1, ws3, ws2 = w_s1[0].astype(BF16), w_s3[0].astype(BF16), w_s2[0].astype(BF16)

    tiles_per_seq = seq // ROW_TILE
    groups_s = ROW_TILE // t_new

    cos_p, sin_p = _rope_tables(jnp.arange(seq, dtype=I32))
    xp3 = x_prompt.reshape(m_p // ROW_TILE, ROW_TILE, d)
    (qa_p, ka_p, va_p, qb_p, kb_p, vb_p, gb_p, ga_p, gm_p, gr_p, kmean_p) = _inproj_call(
        xp3, mod_p, nw1, cos_p, sin_p, w_main, w_gr, groups_per_tile=1, rows_per_group=ROW_TILE,
        mod_of_tile=lambda i: i // tiles_per_seq, cos_of_tile=lambda i: i % tiles_per_seq, with_kmean=True,
        name="inproj_prompt")
    qaug, kaug, vaug = _select_call(qa_p, ka_p, va_p, kmean_p, batch, seq)
    oa_p = _moba_attn_call(qaug, kaug, vaug)
    ob_p, state_p = _gla_call(qb_p, kb_p, vb_p, gr_p, wg, bg, None, n_seq=batch, seq_len=seq, chunk=GLA_CHUNK,
                              chunks_per_step=GLA_CHUNKS_PER_STEP, name="gla_prompt")
    x1_p, h2_p, lg_p = _mixer_call(
        xp3, oa_p, ob_p, gb_p, ga_p, gm_p, mod_p, nw2, gnw, wpa, wpb, wout, wr_t, groups_per_tile=1,
        mod_of_tile=lambda i: i // tiles_per_seq,
        oa_of_tile=lambda i: (i // tiles_per_seq, 0, i % tiles_per_seq, 0), name="mixer_prompt")

    cos_s, sin_s = _rope_tables(past_len + jnp.arange(t_new, dtype=I32))
    cos_s = jnp.tile(cos_s, (groups_s, 1))
    sin_s = jnp.tile(sin_s, (groups_s, 1))
    (qa_s, ka_s, va_s, qb_s, kb_s, vb_s, gb_s, ga_s, gm_s, gr_s) = _inproj_call(
        x_sample, mod_s, nw1, cos_s, sin_s, w_main, w_gr, groups_per_tile=groups_s, rows_per_group=t_new,
        mod_of_tile=lambda i: i, cos_of_tile=lambda i: 0, with_kmean=False, name="inproj_sample")
    pt_flat = page_table.reshape(-1)
    ck3 = cache_k[0].reshape(n_pool, page, A_WIDTH)
    cv3 = cache_v[0].reshape(n_pool, page, A_WIDTH)
    o_t = _paged_attn_call(pt_flat, ck3, cv3, qa_s, ka_s, va_s, n_dec, pages_per_seq, t_new)
    o5 = o_t[:, :, :A_HEADS * t_new].reshape(n_dec, A_HEADS, A_HEAD_DIM, A_HEADS, t_new)
    hh = jnp.arange(A_HEADS)
    oa_s = o5[:, hh, :, hh, :]
    oa_s = oa_s.transpose(0, 1, 3, 2).reshape(1, A_HEADS, m_s, A_HEAD_DIM)
    ob_s, state_s = _gla_call(qb_s, kb_s, vb_s, gr_s, wg, bg, state_gla[0], n_seq=n_dec, seq_len=t_new, chunk=t_new,
                              chunks_per_step=1, name="gla_sample")
    x1_s, h2_s, lg_s = _mixer_call(
        x_sample, oa_s, ob_s, gb_s, ga_s, gm_s, mod_s, nw2, gnw, wpa, wpb, wout, wr_t, groups_per_tile=groups_s,
        mod_of_tile=lambda i: i, oa_of_tile=lambda i: (0, 0, i, 0), name="mixer_sample")

    h2 = jnp.concatenate([h2_p, h2_s], axis=0)
    idx, gw = _route_call(jnp.concatenate([lg_p, lg_s], axis=1), br_lanes)
    group_expert, row_token, row_weight, pos = _dispatch_tables(idx, gw)
    ysorted = _expert_call(group_expert, row_token, row_weight, h2, we1, we3, we2)
    m_all = m_p + m_s
    pos_tiles = pos.reshape(TOP_K, m_all // ROW_TILE, ROW_TILE).transpose(1, 0, 2).reshape(-1)
    n_tiles_p = m_p // ROW_TILE
    y_p = _final_call(pos_tiles[:n_tiles_p * TOP_K * ROW_TILE], x1_p, h2_p, mod_p, nwf, ws1, ws3, ws2, ysorted,
                      groups_per_tile=1, mod_of_tile=lambda i: i // tiles_per_seq, name="final_prompt")
    y_s = _final_call(pos_tiles[n_tiles_p * TOP_K * ROW_TILE:], x1_s, h2_s, mod_s, nwf, ws1, ws3, ws2, ysorted,
                      groups_per_tile=groups_s, mod_of_tile=lambda i: i, name="final_sample")

    return (
        y_p.reshape(batch, seq, d),
        y_s,
        ka_p.reshape(1, batch, seq, A_HEADS, A_HEAD_DIM),
        va_p.reshape(1, batch, seq, A_HEADS, A_HEAD_DIM),
        state_p[None],
        ka_s.reshape(1, n_dec, t_new, A_HEADS, A_HEAD_DIM),
        va_s.reshape(1, n_dec, t_new, A_HEADS, A_HEAD_DIM),
        state_s[None],
    )
```

```python
import functools

import jax
import jax.numpy as jnp
from jax import lax
from jax.experimental import pallas as pl
from jax.experimental.pallas import tpu as pltpu

F32 = jnp.float32
BF16 = jnp.bfloat16
I32 = jnp.int32
HIGHEST = lax.Precision.HIGHEST

A_HEADS = 8
A_HEAD_DIM = 64
A_WIDTH = A_HEADS * A_HEAD_DIM
MOBA_BLOCK = 256
MOBA_TOPK = 3
ROPE_THETA = 10000.0
B_HEADS = 4
GATE_RANK = 16
GATE_NORMALIZER = 16.0
N_EXPERTS = 64
N_GROUPS = 8
GROUP_SIZE = N_EXPERTS // N_GROUPS
TOPK_GROUPS = 4
TOP_K = 8
ROUTED_SCALE = 2.5
EPS = 1e-6
NEG = -1e30
BELOW_NEG = -3e38

LANES = 128
SUBLANES = 8
VMEM_BYTES = 64 * 1024 * 1024

LOG2_E = 1.4426950408889634
ATTN_KEY_CHUNK = 2048
ATTN_HEADS_PER_STEP = 2

ROW_TILE = 256
GLA_CHUNK = 64
GLA_CHUNKS_PER_STEP = 4
MOE_GROUP_ROWS = 256


def _cparams(semantics, vmem_mb):
    return pltpu.CompilerParams(dimension_semantics=semantics, vmem_limit_bytes=vmem_mb * 1024 * 1024)


def _silu(x):
    return x * jax.nn.sigmoid(x)


def _dot(a, b):
    return jnp.dot(a, b, preferred_element_type=F32)


def _dot_nt(a, b, precision=None):
    return lax.dot_general(a, b, (((1,), (1,)), ((), ())), precision=precision, preferred_element_type=F32)


def _dot_tn(a, b):
    return lax.dot_general(a, b, (((0,), (0,)), ((), ())), preferred_element_type=F32)


def _first_argmax(vals, iota, axis, size):
    mx = jnp.max(vals, axis=axis, keepdims=True)
    return jnp.min(jnp.where(vals == mx, iota, size), axis=axis, keepdims=True)


def _store_row_tiles(ref, x):
    rows, d = x.shape
    sub = d // LANES
    for s in range(sub):
        ref[pl.ds(s, rows, stride=sub), :] = x[:, s * LANES:(s + 1) * LANES]


def _load_row_tiles(ref, rows, sub):
    return jnp.concatenate([ref[pl.ds(s, rows, stride=sub), :] for s in range(sub)], axis=1)


def _ada_kernel(c_ref, w_ref, b_ref, o_ref):
    s = _silu(c_ref[...])
    o_ref[...] = jnp.dot(s, w_ref[...], precision=HIGHEST, preferred_element_type=F32) + b_ref[...]


def _ada_call(c_all, w_ada, b_ada):
    n, d = c_all.shape
    width = w_ada.shape[1]
    tn = width // 4
    return pl.pallas_call(
        _ada_kernel,
        grid=(width // tn,),
        in_specs=[
            pl.BlockSpec((n, d), lambda j: (0, 0)),
            pl.BlockSpec((d, tn), lambda j: (0, j)),
            pl.BlockSpec((1, tn), lambda j: (0, j)),
        ],
        out_specs=pl.BlockSpec((n, tn), lambda j: (0, j)),
        out_shape=jax.ShapeDtypeStruct((n, width), F32),
        compiler_params=_cparams(("arbitrary",), 40),
        name="adaln",
    )(c_all, w_ada, b_ada.reshape(1, width))


def _rope(a, cos, sin):
    lane = lax.broadcasted_iota(I32, cos.shape, 1)
    first_half = (lane % A_HEAD_DIM) < (A_HEAD_DIM // 2)
    outs = []
    for c in range(a.shape[1] // LANES):
        xc = a[:, c * LANES:(c + 1) * LANES]
        rot = jnp.where(first_half, -pltpu.roll(xc, LANES - A_HEAD_DIM // 2, 1), pltpu.roll(xc, A_HEAD_DIM // 2, 1))
        outs.append(xc * cos + rot * sin)
    return jnp.concatenate(outs, axis=1)


def _modulated_norm(x, norm_w, scale, shift):
    ms = jnp.mean(x * x, axis=-1, keepdims=True)
    y = x * lax.rsqrt(ms + EPS) * norm_w
    return y * (1.0 + scale) + shift


def _inproj_kernel(x_ref, sh_ref, sc_ref, nw_ref, cos_ref, sin_ref, w_ref, wgr_ref,
                   q_ref, k_ref, v_ref, qb_ref, kb_ref, vb_ref, gb_ref, ga_ref, gm_ref, gr_ref, *km_refs):
    g, r, d = x_ref.shape
    h = _modulated_norm(x_ref[...], nw_ref[...], sc_ref[...], sh_ref[...])
    hb = h.reshape(g * r, d).astype(BF16)
    cos = cos_ref[...]
    sin = sin_ref[...]
    col = 0

    def proj(width):
        nonlocal col
        out = _dot(hb, w_ref[:, col:col + width])
        col += width
        return out

    q_ref[...] = _rope(proj(A_WIDTH), cos, sin)
    k = _rope(proj(A_WIDTH), cos, sin)
    k_ref[...] = k
    if km_refs:
        nb = (g * r) // MOBA_BLOCK
        km_refs[0][...] = jnp.mean(k.reshape(nb, MOBA_BLOCK, A_WIDTH), axis=1, keepdims=True)
    v_ref[...] = proj(A_WIDTH)
    for ref in (qb_ref, kb_ref, vb_ref, gb_ref, ga_ref, gm_ref):
        ref[...] = proj(ref.shape[1])
    gr_ref[...] = _dot(hb, wgr_ref[...])


def _inproj_call(x3, mod3, norm_w, cos, sin, w_main, w_gr, *, groups_per_tile, rows_per_group, mod_of_tile,
                 cos_of_tile, with_kmean, name):
    n_groups, r, d = x3.shape
    assert r == rows_per_group
    m = n_groups * r
    tm = groups_per_tile * r
    n_tiles = n_groups // groups_per_tile
    d_model = d
    bk = w_main.shape[1] - 3 * A_WIDTH - 4 * d_model
    widths = [A_WIDTH, A_WIDTH, A_WIDTH, bk // 2, bk // 2, d_model, d_model, d_model, d_model, LANES]
    out_shape = [jax.ShapeDtypeStruct((m, w), F32) for w in widths]
    out_specs = [pl.BlockSpec((tm, w), lambda i: (i, 0)) for w in widths]
    if with_kmean:
        out_shape.append(jax.ShapeDtypeStruct((m // MOBA_BLOCK, 1, A_WIDTH), F32))
        out_specs.append(pl.BlockSpec((tm // MOBA_BLOCK, 1, A_WIDTH), lambda i: (i, 0, 0)))
    resident = dict(pipeline_mode=pl.Buffered(1))
    return pl.pallas_call(
        _inproj_kernel,
        grid=(n_tiles,),
        in_specs=[
            pl.BlockSpec((groups_per_tile, r, d), lambda i: (i, 0, 0)),
            pl.BlockSpec((groups_per_tile, 1, d), lambda i: (mod_of_tile(i), 0, 0)),
            pl.BlockSpec((groups_per_tile, 1, d), lambda i: (mod_of_tile(i), 0, 1)),
            pl.BlockSpec((1, d), lambda i: (0, 0), **resident),
            pl.BlockSpec((tm, LANES), lambda i: (cos_of_tile(i), 0)),
            pl.BlockSpec((tm, LANES), lambda i: (cos_of_tile(i), 0)),
            pl.BlockSpec(w_main.shape, lambda i: (0, 0), **resident),
            pl.BlockSpec(w_gr.shape, lambda i: (0, 0), **resident),
        ],
        out_specs=out_specs,
        out_shape=out_shape,
        compiler_params=_cparams(("parallel",), 48),
        name=name,
    )(x3, mod3, mod3, norm_w, cos, sin, w_main, w_gr)


def _select_kernel(q_ref, k_ref, v_ref, km_ref, qa_ref, ka_ref, va_ref, *, nbw):
    i = pl.program_id(1)
    q = q_ref[...]
    k = k_ref[...]
    v = v_ref[...]
    km = km_ref[:, 0, :]
    nblk = km.shape[0]
    rows = q.shape[0]
    q_t = q.T
    v_t = v.T
    blk = lax.broadcasted_iota(I32, (nblk, rows), 0)
    colw = lax.broadcasted_iota(I32, (rows, nbw), 1)
    ones_row = (lax.broadcasted_iota(I32, (A_HEAD_DIM, rows), 0) == 0).astype(F32)
    own_onehot = jnp.where(colw == i, -NEG, 0.0)
    for h in range(A_HEADS):
        sl = slice(h * A_HEAD_DIM, (h + 1) * A_HEAD_DIM)
        s = _dot_nt(km[:, sl], q[:, sl], precision=HIGHEST)
        s = jnp.where(blk < i, s, NEG)
        sel = jnp.zeros(s.shape, jnp.bool_)
        for _ in range(MOBA_TOPK):
            first = _first_argmax(s, blk, 0, nblk)
            hit = blk == first
            sel = sel | (hit & (first < i))
            s = jnp.where(hit, BELOW_NEG, s)
        selneg = jnp.where(sel, 0.0, -1.0)
        if nbw > nblk:
            selneg = jnp.concatenate([selneg, jnp.full((nbw - nblk, rows), -1.0, F32)], axis=0)
        qa_ref[0, h] = jnp.concatenate([q_t[sl, :] * (A_HEAD_DIM ** -0.5 * LOG2_E), selneg], axis=0).astype(BF16)
        ka_ref[0, h] = jnp.concatenate([k[:, sl], own_onehot], axis=1).astype(BF16)
        va_ref[0, h] = jnp.concatenate([v_t[sl, :], ones_row], axis=0).astype(BF16)


def _select_call(q, k, v, kmean, batch, seq):
    nblk = seq // MOBA_BLOCK
    nbw = -(-nblk // A_HEAD_DIM) * A_HEAD_DIM
    aw = A_HEAD_DIM + nbw
    row_spec = pl.BlockSpec((MOBA_BLOCK, A_WIDTH), lambda b, i: (b * nblk + i, 0))
    t_spec = lambda w: pl.BlockSpec((1, A_HEADS, w, MOBA_BLOCK), lambda b, i: (b, 0, 0, i))
    return pl.pallas_call(
        functools.partial(_select_kernel, nbw=nbw),
        grid=(batch, nblk),
        in_specs=[row_spec, row_spec, row_spec, pl.BlockSpec((nblk, 1, A_WIDTH), lambda b, i: (b, 0, 0))],
        out_specs=[t_spec(aw), pl.BlockSpec((1, A_HEADS, MOBA_BLOCK, aw), lambda b, i: (b, 0, i, 0)),
                   t_spec(2 * A_HEAD_DIM)],
        out_shape=[
            jax.ShapeDtypeStruct((batch, A_HEADS, aw, seq), BF16),
            jax.ShapeDtypeStruct((batch, A_HEADS, seq, aw), BF16),
            jax.ShapeDtypeStruct((batch, A_HEADS, 2 * A_HEAD_DIM, seq), BF16),
        ],
        compiler_params=_cparams(("parallel", "parallel"), 32),
        name="moba_select",
    )(q, k, v, kmean)


def _moba_attn_kernel(qa_ref, ka_ref, va_ref, kown_ref, vown_ref, o_ref, s_ref, m_ref, acc_ref):
    i = pl.program_id(2)
    chunk = ATTN_KEY_CHUNK
    heads = range(qa_ref.shape[1])
    m_ref[...] = jnp.full(m_ref.shape, BELOW_NEG, F32)
    acc_ref[...] = jnp.zeros(acc_ref.shape, F32)

    def flash_step(h, s_t, v_t):
        m_prev = m_ref[h, 0:1, :]
        m_new = jnp.maximum(m_prev, jnp.max(s_t, axis=0, keepdims=True))
        alpha = jnp.exp2(m_prev - m_new)
        p_t = jnp.exp2(s_t - m_new).astype(BF16)
        acc_ref[h] = alpha * acc_ref[h] + _dot(v_t, p_t)
        m_ref[h] = jnp.broadcast_to(m_new, m_ref.shape[1:])

    n_trips = (i * MOBA_BLOCK + chunk - 1) // chunk
    last_chunk = ka_ref.shape[2] // chunk - 1

    def logits(h, c):
        r0 = pl.multiple_of(c * chunk, chunk)
        return _dot(ka_ref[0, h, pl.ds(r0, chunk), :], qa_ref[0, h])

    @pl.when(n_trips > 0)
    def _():
        for h in heads:
            s_ref[0, h] = logits(h, 0)

    def past_chunk(c, carry):
        slot = c % 2
        for h in heads:
            s_ref[1 - slot, h] = logits(h, jnp.minimum(c + 1, last_chunk))
        r0 = pl.multiple_of(c * chunk, chunk)
        for h in heads:
            flash_step(h, s_ref[slot, h], va_ref[0, h, :, pl.ds(r0, chunk)])
        return carry

    lax.fori_loop(0, n_trips, past_chunk, 0)
    for h in heads:
        s_t = _dot(kown_ref[0, h][:, 0:A_HEAD_DIM], qa_ref[0, h][0:A_HEAD_DIM, :])
        key = lax.broadcasted_iota(I32, s_t.shape, 0)
        qry = lax.broadcasted_iota(I32, s_t.shape, 1)
        flash_step(h, jnp.where(key <= qry, s_t, NEG), vown_ref[0, h])
        acc = acc_ref[h].T
        o_ref[0, h] = acc[:, :A_HEAD_DIM] / acc[:, A_HEAD_DIM:A_HEAD_DIM + 1]


def _moba_attn_call(qaug_t, kaug, vaug_t):
    batch, heads, seq, aw = kaug.shape
    nblk = seq // MOBA_BLOCK
    vw = vaug_t.shape[2]
    hs = ATTN_HEADS_PER_STEP
    return pl.pallas_call(
        _moba_attn_kernel,
        grid=(batch, heads // hs, nblk),
        in_specs=[
            pl.BlockSpec((1, hs, aw, MOBA_BLOCK), lambda b, h, i: (b, h, 0, i)),
            pl.BlockSpec((1, hs, seq, aw), lambda b, h, i: (b, h, 0, 0)),
            pl.BlockSpec((1, hs, vw, seq), lambda b, h, i: (b, h, 0, 0)),
            pl.BlockSpec((1, hs, MOBA_BLOCK, aw), lambda b, h, i: (b, h, i, 0)),
            pl.BlockSpec((1, hs, vw, MOBA_BLOCK), lambda b, h, i: (b, h, 0, i)),
        ],
        out_specs=pl.BlockSpec((1, hs, MOBA_BLOCK, A_HEAD_DIM), lambda b, h, i: (b, h, i, 0)),
        out_shape=jax.ShapeDtypeStruct((batch, heads, seq, A_HEAD_DIM), F32),
        scratch_shapes=[pltpu.VMEM((2, hs, ATTN_KEY_CHUNK, MOBA_BLOCK), F32),
                        pltpu.VMEM((hs, SUBLANES, MOBA_BLOCK), F32), pltpu.VMEM((hs, vw, MOBA_BLOCK), F32)],
        compiler_params=_cparams(("parallel", "parallel", "arbitrary"), 56),
        name="moba_attention",
    )(qaug_t, kaug, vaug_t, kaug, vaug_t)


def _gla_kernel(*refs, chunk, n_chunks, has_s0):
    if has_s0:
        q_ref, k_ref, v_ref, gr_ref, wg_ref, bg_ref, s0_ref, o_ref, sout_ref, st_ref = refs
    else:
        q_ref, k_ref, v_ref, gr_ref, wg_ref, bg_ref, o_ref, sout_ref, st_ref = refs
    step = pl.program_id(2)
    dk = q_ref.shape[1]

    @pl.when(step == 0)
    def _():
        if has_s0:
            st_ref[...] = s0_ref[0, 0].T
        else:
            st_ref[...] = jnp.zeros(st_ref.shape, F32)

    rows = lax.broadcasted_iota(I32, (chunk, chunk), 0)
    cols = lax.broadcasted_iota(I32, (chunk, chunk), 1)
    causal = cols <= rows
    tri = causal.astype(F32)
    wg = wg_ref[0]
    bg = bg_ref[0]
    for c in range(n_chunks):
        sl = slice(c * chunk, (c + 1) * chunk)
        q = q_ref[sl, :] * (dk ** -0.5)
        k = k_ref[sl, :]
        v = v_ref[sl, :].astype(BF16)
        z = jnp.dot(gr_ref[sl, :], wg, precision=HIGHEST, preferred_element_type=F32) + bg
        log_a = -(jnp.maximum(-z, 0.0) + jnp.log1p(jnp.exp(-jnp.abs(z)))) * (1.0 / GATE_NORMALIZER)
        b = jnp.dot(tri, log_a, precision=HIGHEST, preferred_element_type=F32)
        b_last = b[chunk - 1:chunk, :]
        qe = (q * jnp.exp(b)).astype(BF16)
        ke = (k * jnp.exp(-b)).astype(BF16)
        kd = (k * jnp.exp(b_last - b)).astype(BF16)
        st = st_ref[...]
        scores = jnp.where(causal, _dot_nt(qe, ke), 0.0)
        o_ref[sl, :] = _dot(scores.astype(BF16), v) + _dot_nt(qe, st.astype(BF16))
        st_ref[...] = st * jnp.exp(b_last) + _dot_tn(v, kd)

    @pl.when(step == pl.num_programs(2) - 1)
    def _():
        sout_ref[0, 0] = st_ref[...].T


def _gla_call(qb, kb, vb, gr, wg, bg, s0, *, n_seq, seq_len, chunk, chunks_per_step, name):
    heads = B_HEADS
    dk = qb.shape[1] // heads
    dv = vb.shape[1] // heads
    rb = chunk * chunks_per_step
    steps = seq_len // rb
    row = lambda n, h, s: n * steps + s
    in_specs = [
        pl.BlockSpec((rb, dk), lambda n, h, s: (row(n, h, s), h)),
        pl.BlockSpec((rb, dk), lambda n, h, s: (row(n, h, s), h)),
        pl.BlockSpec((rb, dv), lambda n, h, s: (row(n, h, s), h)),
        pl.BlockSpec((rb, LANES), lambda n, h, s: (row(n, h, s), 0)),
        pl.BlockSpec((1, LANES, dk), lambda n, h, s: (h, 0, 0)),
        pl.BlockSpec((1, 1, dk), lambda n, h, s: (h, 0, 0)),
    ]
    args = [qb, kb, vb, gr, wg, bg]
    if s0 is not None:
        in_specs.append(pl.BlockSpec((1, 1, dk, dv), lambda n, h, s: (n, h, 0, 0)))
        args.append(s0)
    return pl.pallas_call(
        functools.partial(_gla_kernel, chunk=chunk, n_chunks=chunks_per_step, has_s0=s0 is not None),
        grid=(n_seq, heads, steps),
        in_specs=in_specs,
        out_specs=[
            pl.BlockSpec((rb, dv), lambda n, h, s: (row(n, h, s), h)),
            pl.BlockSpec((1, 1, dk, dv), lambda n, h, s: (n, h, 0, 0)),
        ],
        out_shape=[
            jax.ShapeDtypeStruct((n_seq * seq_len, heads * dv), F32),
            jax.ShapeDtypeStruct((n_seq, heads, dk, dv), F32),
        ],
        scratch_shapes=[pltpu.VMEM((dv, dk), F32)],
        compiler_params=_cparams(("parallel", "parallel", "arbitrary"), 32),
        name=name,
    )(*args)


def _mixer_kernel(x_ref, oa_ref, ob_ref, gb_ref, ga_ref, gm_ref, g1_ref, sh2_ref, sc2_ref, nw2_ref, gnw_ref,
                  wpa_ref, wpb_ref, wout_ref, wr_ref, x1_ref, h2_ref, lg_ref):
    g, r, d = x_ref.shape
    tm = g * r
    dv = gnw_ref.shape[1]
    ob = ob_ref[...]
    gb = gb_ref[...]
    gnw = gnw_ref[...]
    parts = []
    for h in range(ob.shape[1] // dv):
        oh = ob[:, h * dv:(h + 1) * dv]
        ms = jnp.mean(oh * oh, axis=-1, keepdims=True)
        parts.append((oh * lax.rsqrt(ms + EPS) * gnw * _silu(gb[:, h * dv:(h + 1) * dv])).astype(BF16))
    pb = _dot(jnp.concatenate(parts, axis=1), wpb_ref[...])
    pa = jnp.zeros((tm, d), F32)
    for h in range(A_HEADS):
        pa = pa + _dot(oa_ref[0, h].astype(BF16), wpa_ref[h])
    merged = jax.nn.sigmoid(ga_ref[...]) * pa + jax.nn.sigmoid(gm_ref[...]) * pb
    u = _dot(merged.astype(BF16), wout_ref[...])
    x1 = x_ref[...] + g1_ref[...] * u.reshape(g, r, d)
    x1_ref[...] = x1
    h2 = _modulated_norm(x1, nw2_ref[...], sc2_ref[...], sh2_ref[...]).reshape(tm, d)
    _store_row_tiles(h2_ref, h2)
    lg_ref[...] = _dot_nt(wr_ref[...], h2, precision=HIGHEST)


def _mixer_call(x3, oa4, ob, gb, ga, gm, mod3, nw2, gnw, wpa, wpb, wout, wr_t, *, groups_per_tile, mod_of_tile,
                oa_of_tile, name):
    n_groups, r, d = x3.shape
    m = n_groups * r
    tm = groups_per_tile * r
    n_tiles = n_groups // groups_per_tile
    row = lambda w: pl.BlockSpec((tm, w), lambda i: (i, 0))
    modspec = lambda k: pl.BlockSpec((groups_per_tile, 1, d), lambda i: (mod_of_tile(i), 0, k))
    const = lambda a: pl.BlockSpec(a.shape, lambda i: (0,) * a.ndim, pipeline_mode=pl.Buffered(1))
    return pl.pallas_call(
        _mixer_kernel,
        grid=(n_tiles,),
        in_specs=[
            pl.BlockSpec((groups_per_tile, r, d), lambda i: (i, 0, 0)),
            pl.BlockSpec((1, A_HEADS, tm, A_HEAD_DIM), lambda i: oa_of_tile(i)),
            row(ob.shape[1]), row(gb.shape[1]), row(d), row(d),
            modspec(2), modspec(3), modspec(4),
            const(nw2), const(gnw), const(wpa), const(wpb), const(wout), const(wr_t),
        ],
        out_specs=[
            pl.BlockSpec((groups_per_tile, r, d), lambda i: (i, 0, 0)),
            pl.BlockSpec((tm * d // LANES, LANES), lambda i: (i, 0)),
            pl.BlockSpec((N_EXPERTS, tm), lambda i: (0, i)),
        ],
        out_shape=[
            jax.ShapeDtypeStruct((n_groups, r, d), F32),
            jax.ShapeDtypeStruct((m * d // LANES, LANES), F32),
            jax.ShapeDtypeStruct((N_EXPERTS, m), F32),
        ],
        compiler_params=_cparams(("parallel",), 48),
        name=name,
    )(x3, oa4, ob, gb, ga, gm, mod3, mod3, mod3, nw2, gnw, wpa, wpb, wout, wr_t)


def _route_kernel(lg_ref, br_ref, idx_ref, gw_ref, rank_ref, cnt_ref, carry_ref):
    @pl.when(pl.program_id(0) == 0)
    def _():
        carry_ref[...] = jnp.zeros(carry_ref.shape, F32)

    logits = lg_ref[...]
    n_e, tm = logits.shape
    scores = jax.nn.sigmoid(logits)
    biased = scores + jnp.concatenate([br_ref[...]] * (tm // LANES), axis=1)
    g3 = biased.reshape(N_GROUPS, GROUP_SIZE, tm)
    sub = lax.broadcasted_iota(I32, g3.shape, 1)
    m1 = jnp.max(g3, axis=1, keepdims=True)
    f1 = jnp.min(jnp.where(g3 == m1, sub, GROUP_SIZE), axis=1, keepdims=True)
    m2 = jnp.max(jnp.where(sub == f1, -jnp.inf, g3), axis=1, keepdims=True)
    gs = (m1 + m2).reshape(N_GROUPS, tm)
    giota = lax.broadcasted_iota(I32, gs.shape, 0)
    gmask = jnp.zeros(gs.shape, jnp.bool_)
    for _ in range(TOPK_GROUPS):
        first = _first_argmax(gs, giota, 0, N_GROUPS)
        hit = giota == first
        gmask = gmask | hit
        gs = jnp.where(hit, -jnp.inf, gs)
    emask = jnp.broadcast_to(gmask.reshape(N_GROUPS, 1, tm), (N_GROUPS, GROUP_SIZE, tm)).reshape(n_e, tm)
    masked = jnp.where(emask, biased, -jnp.inf)
    eiota = lax.broadcasted_iota(I32, masked.shape, 0)
    idx_rows, w_rows = [], []
    for _ in range(TOP_K):
        first = _first_argmax(masked, eiota, 0, n_e)
        hit = eiota == first
        idx_rows.append(first)
        w_rows.append(jnp.sum(jnp.where(hit, scores, 0.0), axis=0, keepdims=True))
        masked = jnp.where(hit, -jnp.inf, masked)
    w = jnp.concatenate(w_rows, axis=0)
    idx_ref[...] = jnp.concatenate(idx_rows, axis=0)
    gw_ref[...] = w / jnp.sum(w, axis=0, keepdims=True) * ROUTED_SCALE
    chosen = jnp.zeros(masked.shape, F32)
    for first in idx_rows:
        chosen = chosen + (eiota == first).astype(F32)
    earlier = (lax.broadcasted_iota(I32, (tm, tm), 0) < lax.broadcasted_iota(I32, (tm, tm), 1)).astype(BF16)
    carry = carry_ref[...]
    base = _dot(chosen.astype(BF16), earlier) + jnp.concatenate([carry] * (tm // LANES), axis=1)
    rank_ref[...] = jnp.concatenate(
        [jnp.sum(jnp.where(eiota == first, base, 0.0), axis=0, keepdims=True) for first in idx_rows], axis=0).astype(I32)
    carry = carry + jnp.sum(chosen, axis=1, keepdims=True)
    carry_ref[...] = carry
    cnt_ref[...] = carry.astype(I32)


def _route_call(logits_t, b_router_lanes):
    n_e, m = logits_t.shape
    tm = 512 if m % 512 == 0 else ROW_TILE
    tok = lambda dt: jax.ShapeDtypeStruct((TOP_K, m), dt)
    tok_spec = pl.BlockSpec((TOP_K, tm), lambda i: (0, i))
    return pl.pallas_call(
        _route_kernel,
        grid=(m // tm,),
        in_specs=[pl.BlockSpec((n_e, tm), lambda i: (0, i)), pl.BlockSpec((n_e, LANES), lambda i: (0, 0))],
        out_specs=[tok_spec, tok_spec, tok_spec, pl.BlockSpec((n_e, LANES), lambda i: (0, 0))],
        out_shape=[tok(I32), tok(F32), tok(I32), jax.ShapeDtypeStruct((n_e, LANES), I32)],
        scratch_shapes=[pltpu.VMEM((n_e, LANES), F32)],
        compiler_params=_cparams(("arbitrary",), 32),
        name="moe_route",
    )(logits_t, b_router_lanes)


DMA_ISSUE_UNROLL = 8


def _dispatch_kernel(cnt_ref, pstart_ref, nu_ref, pos_ref, h2g_ref, xs_hbm, zero_ref, sem, zsem, gsem, *, tm, sub):
    i = pl.program_id(0)
    group_sub = MOE_GROUP_ROWS * sub

    @pl.when(i == 0)
    def _():
        zero_ref[...] = jnp.zeros(zero_ref.shape, F32)
        zero_row = zero_ref.at[pl.ds(0, sub), :]

        def per_expert(e, n_zero):
            count = cnt_ref[e]
            padded = (count + MOE_GROUP_ROWS - 1) // MOE_GROUP_ROWS * MOE_GROUP_ROWS
            base = pstart_ref[e]

            def fill(r, carry):
                dst = xs_hbm.at[pl.ds(pl.multiple_of((base + r) * sub, sub), sub), :]
                pltpu.make_async_copy(zero_row, dst, zsem).start()
                return carry

            lax.fori_loop(count, padded, fill, 0)
            return n_zero + (padded - count)

        n_zero = lax.fori_loop(0, N_EXPERTS, per_expert, 0)
        n_groups = xs_hbm.shape[0] // group_sub

        def fill_group(g, carry):
            dst = xs_hbm.at[pl.ds(pl.multiple_of(g * group_sub, group_sub), group_sub), :]
            pltpu.make_async_copy(zero_ref, dst, gsem).start()
            return carry

        lax.fori_loop(nu_ref[0], n_groups, fill_group, 0)

        def drain(r, carry):
            pltpu.make_async_copy(zero_row, xs_hbm.at[pl.ds(0, sub), :], zsem).wait()
            return carry

        lax.fori_loop(0, n_zero, drain, 0)

        def drain_group(g, carry):
            pltpu.make_async_copy(zero_ref, xs_hbm.at[pl.ds(0, group_sub), :], gsem).wait()
            return carry

        lax.fori_loop(nu_ref[0], n_groups, drain_group, 0)

    def issue(tb, carry):
        for u in range(DMA_ISSUE_UNROLL):
            t = tb * DMA_ISSUE_UNROLL + u
            src = h2g_ref.at[pl.ds(pl.multiple_of(t * sub, sub), sub), :]
            for k in range(TOP_K):
                p = pl.multiple_of(pos_ref[k * tm + t], sub)
                pltpu.make_async_copy(src, xs_hbm.at[pl.ds(p, sub), :], sem).start()
        return carry

    lax.fori_loop(0, tm // DMA_ISSUE_UNROLL, issue, 0)
    for _ in range(TOP_K):
        pltpu.make_async_copy(h2g_ref, xs_hbm.at[pl.ds(0, tm * sub), :], sem).wait()


def _dispatch_call(counts, pad_start, n_used, pos_tiles, h2g, n_rows_sorted, d):
    sub = d // LANES
    tm = ROW_TILE
    m = h2g.shape[0] // sub
    grid_spec = pltpu.PrefetchScalarGridSpec(
        num_scalar_prefetch=3,
        grid=(m // tm,),
        in_specs=[
            pl.BlockSpec((TOP_K * tm,), lambda i, c, p, u: (i,), memory_space=pltpu.SMEM),
            pl.BlockSpec((tm * sub, LANES), lambda i, c, p, u: (i, 0)),
        ],
        out_specs=pl.BlockSpec(memory_space=pl.ANY),
        scratch_shapes=[pltpu.VMEM((MOE_GROUP_ROWS * sub, LANES), F32), pltpu.SemaphoreType.DMA(()),
                        pltpu.SemaphoreType.DMA(()), pltpu.SemaphoreType.DMA(())],
    )
    return pl.pallas_call(
        functools.partial(_dispatch_kernel, tm=tm, sub=sub),
        grid_spec=grid_spec,
        out_shape=jax.ShapeDtypeStruct((n_rows_sorted * sub, LANES), F32),
        compiler_params=_cparams(("arbitrary",), 32),
        name="moe_dispatch",
    )(counts, pad_start, n_used, pos_tiles, h2g)


def _expert_kernel(ge_ref, nu_ref, x_ref, w1_ref, w3_ref, w2_ref, y_ref):
    g = pl.program_id(0)
    rows = MOE_GROUP_ROWS
    sub = x_ref.shape[0] // rows

    @pl.when(g < nu_ref[0])
    def _():
        xb = _load_row_tiles(x_ref, rows, sub).astype(BF16)
        hidden = _silu(_dot(xb, w1_ref[0])) * _dot(xb, w3_ref[0])
        _store_row_tiles(y_ref, _dot(hidden.astype(BF16), w2_ref[0]))

    @pl.when(g >= nu_ref[0])
    def _():
        y_ref[...] = jnp.zeros(y_ref.shape, F32)


def _expert_call(group_expert, n_used, xs, w1, w3, w2):
    n_groups = group_expert.shape[0]
    rows = MOE_GROUP_ROWS
    _, d, hidden = w1.shape
    sub = d // LANES
    used = lambda g, nu: jnp.minimum(g, nu[0] - 1)
    grid_spec = pltpu.PrefetchScalarGridSpec(
        num_scalar_prefetch=2,
        grid=(n_groups,),
        in_specs=[
            pl.BlockSpec((rows * sub, LANES), lambda g, ge, nu: (used(g, nu), 0)),
            pl.BlockSpec((1, d, hidden), lambda g, ge, nu: (ge[used(g, nu)], 0, 0)),
            pl.BlockSpec((1, d, hidden), lambda g, ge, nu: (ge[used(g, nu)], 0, 0)),
            pl.BlockSpec((1, hidden, d), lambda g, ge, nu: (ge[used(g, nu)], 0, 0)),
        ],
        out_specs=pl.BlockSpec((rows * sub, LANES), lambda g, ge, nu: (g, 0)),
    )
    return pl.pallas_call(
        _expert_kernel,
        grid_spec=grid_spec,
        out_shape=jax.ShapeDtypeStruct((n_groups * rows * sub, LANES), F32),
        compiler_params=_cparams(("arbitrary",), 40),
        name="moe_experts",
    )(group_expert, n_used, xs, w1, w3, w2)


def _start_tile_gather(idx_ref, n_rows, sub, src_hbm, dst, sem):
    def issue(rb, carry):
        for u in range(DMA_ISSUE_UNROLL):
            r = rb * DMA_ISSUE_UNROLL + u
            p = pl.multiple_of(idx_ref[r], sub)
            pltpu.make_async_copy(src_hbm.at[pl.ds(p, sub), :],
                                  dst.at[pl.ds(pl.multiple_of(r * sub, sub), sub), :], sem).start()
        return carry

    lax.fori_loop(0, n_rows // DMA_ISSUE_UNROLL, issue, 0)


def _final_kernel(pos_ref, pos_next_ref, gw_ref, x1_ref, h2g_ref, g2_ref, nwf_ref, ws1_ref, ws3_ref, ws2_ref, ys_hbm,
                  y_ref, buf, sem):
    i = pl.program_id(0)
    n_i = pl.num_programs(0)
    g, r, d = x1_ref.shape
    tm = g * r
    sub = d // LANES
    n_rows = TOP_K * tm
    slot = i % 2

    @pl.when(i == 0)
    def _():
        _start_tile_gather(pos_ref, n_rows, sub, ys_hbm, buf.at[0], sem.at[0])

    @pl.when(i + 1 < n_i)
    def _():
        _start_tile_gather(pos_next_ref, n_rows, sub, ys_hbm, buf.at[1 - slot], sem.at[1 - slot])

    hb = _load_row_tiles(h2g_ref, tm, sub).astype(BF16)
    shared = _dot((_silu(_dot(hb, ws1_ref[...])) * _dot(hb, ws3_ref[...])).astype(BF16), ws2_ref[...])
    gw_t = jnp.concatenate([gw_ref[...], jnp.zeros((LANES - TOP_K, tm), F32)], axis=0).T
    gates = [jnp.broadcast_to(gw_t[:, k:k + 1], (tm, LANES)) for k in range(TOP_K)]
    pltpu.make_async_copy(ys_hbm.at[pl.ds(0, n_rows * sub), :], buf.at[slot], sem.at[slot]).wait()
    parts = []
    for s in range(sub):
        acc = None
        for k in range(TOP_K):
            term = buf[slot, pl.ds(k * tm * sub + s, tm, stride=sub), :] * gates[k]
            acc = term if acc is None else acc + term
        parts.append(acc)
    routed = jnp.concatenate(parts, axis=1)
    x2 = x1_ref[...] + g2_ref[...] * (shared + routed).reshape(g, r, d)
    ms = jnp.mean(x2 * x2, axis=-1, keepdims=True)
    y_ref[...] = x2 * lax.rsqrt(ms + EPS) * nwf_ref[...]


def _final_call(pos_tiles, gw, x3, h2g, mod3, nwf, ws1, ws3, ws2, ysorted, *, groups_per_tile, mod_of_tile,
                first_tile, name):
    n_groups, r, d = x3.shape
    tm = groups_per_tile * r
    sub = d // LANES
    n_tiles = n_groups // groups_per_tile
    n_rows = TOP_K * tm
    const = lambda a: pl.BlockSpec(a.shape, lambda i: (0,) * a.ndim, pipeline_mode=pl.Buffered(1))
    return pl.pallas_call(
        _final_kernel,
        grid=(n_tiles,),
        in_specs=[
            pl.BlockSpec((n_rows,), lambda i: (first_tile + i,), memory_space=pltpu.SMEM),
            pl.BlockSpec((n_rows,), lambda i: (first_tile + jnp.minimum(i + 1, n_tiles - 1),),
                         memory_space=pltpu.SMEM),
            pl.BlockSpec((TOP_K, tm), lambda i: (0, first_tile + i)),
            pl.BlockSpec((groups_per_tile, r, d), lambda i: (i, 0, 0)),
            pl.BlockSpec((tm * sub, LANES), lambda i: (first_tile + i, 0)),
            pl.BlockSpec((groups_per_tile, 1, d), lambda i: (mod_of_tile(i), 0, 5)),
            const(nwf), const(ws1), const(ws3), const(ws2),
            pl.BlockSpec(memory_space=pl.ANY),
        ],
        out_specs=pl.BlockSpec((groups_per_tile, r, d), lambda i: (i, 0, 0)),
        out_shape=jax.ShapeDtypeStruct((n_groups, r, d), F32),
        scratch_shapes=[pltpu.VMEM((2, n_rows * sub, LANES), F32), pltpu.SemaphoreType.DMA((2,))],
        compiler_params=_cparams(("arbitrary",), 56),
        name=name,
    )(pos_tiles, pos_tiles, gw, x3, h2g, mod3, nwf, ws1, ws3, ws2, ysorted)


def _paged_attn_kernel(pt_ref, ck_hbm, cv_hbm, q_ref, kn_ref, vn_ref, o_ref,
                       cbuf, sem, qbd_ref, s_ref, p_ref, l_ref, acc_ref, *, pages_per_seq, n_blocks, n_sel):
    n = pl.program_id(0)
    u = pl.program_id(1)
    n_seq = pl.num_programs(0)
    n_u = pl.num_programs(1)
    n_kc = n_u // 2
    _, pages_step, heads, hd, page = cbuf.shape
    width = heads * hd
    rows_step = pages_step * page
    blocks_step = rows_step // MOBA_BLOCK
    t_new = q_ref.shape[0]
    n_cols = heads * t_new
    slot = (n * n_u + u) % 2

    def start_chunk(nn, uu, s):
        def copy_pages(src_hbm, cc):
            for p in range(pages_step):
                page_id = pt_ref[nn * pages_per_seq + cc * pages_step + p]
                pltpu.make_async_copy(src_hbm.at[page_id], cbuf.at[s, p], sem.at[s]).start()

        @pl.when(uu < n_kc)
        def _():
            copy_pages(ck_hbm, uu)

        @pl.when(uu >= n_kc)
        def _():
            copy_pages(cv_hbm, uu - n_kc)

    @pl.when((n == 0) & (u == 0))
    def _():
        start_chunk(n, u, slot)

    last_u = u == n_u - 1

    @pl.when(jnp.logical_not(last_u & (n == n_seq - 1)))
    def _():
        start_chunk(jnp.where(last_u, n + 1, n), jnp.where(last_u, 0, u + 1), 1 - slot)

    @pl.when(u == 0)
    def _():
        q = q_ref[...] * (A_HEAD_DIM ** -0.5)
        qt = jnp.concatenate([q] * heads, axis=0)
        r_i = lax.broadcasted_iota(I32, qt.shape, 0)
        c_i = lax.broadcasted_iota(I32, qt.shape, 1)
        qbd = jnp.where(c_i // hd == r_i // t_new, qt, 0.0)
        hi = qbd.astype(BF16)
        qbd_ref[...] = jnp.concatenate([hi, (qbd - hi.astype(F32)).astype(BF16)], axis=0)

    def logits(k_t):
        s2 = _dot(qbd_ref[...], k_t)
        return s2[0:n_cols] + s2[n_cols:2 * n_cols]

    pltpu.make_async_copy(ck_hbm.at[pl.ds(0, pages_step)], cbuf.at[slot], sem.at[slot]).wait()
    chunk_t = jnp.concatenate([cbuf[slot, p].reshape(width, page) for p in range(pages_step)], axis=1).astype(BF16)

    @pl.when(u < n_kc)
    def _():
        s_ref[u] = logits(chunk_t)

    def block_logits(ref, j):
        c, off = divmod(j, blocks_step)
        return ref[c, :, off * MOBA_BLOCK:(off + 1) * MOBA_BLOCK]

    @pl.when(u == n_kc - 1)
    def _():
        lane = lax.broadcasted_iota(I32, (n_cols, LANES), 1)
        sc = jnp.full((n_cols, LANES), NEG, F32)
        for j in range(n_blocks):
            sc = jnp.where(lane == j, jnp.sum(block_logits(s_ref, j), axis=1, keepdims=True), sc)
        sel = jnp.zeros(sc.shape, jnp.bool_)
        for _ in range(n_sel):
            first = _first_argmax(sc, lane, 1, LANES)
            hit = lane == first
            sel = sel | hit
            sc = jnp.where(hit, BELOW_NEG, sc)
        keep = jnp.where(sel, 0.0, NEG)
        bias = [jnp.max(jnp.where(lane == j, keep, NEG), axis=1, keepdims=True) for j in range(n_blocks)]
        s_new = lax.dot_general(qbd_ref[...], kn_ref[...].astype(BF16), (((1,), (1,)), ((), ())),
                                preferred_element_type=F32)
        s_new = s_new[0:n_cols] + s_new[n_cols:2 * n_cols]
        r_i = lax.broadcasted_iota(I32, s_new.shape, 0)
        c_i = lax.broadcasted_iota(I32, s_new.shape, 1)
        s_new = jnp.where(c_i <= r_i % t_new, s_new, NEG)
        m = jnp.max(s_new, axis=1, keepdims=True)
        for j in range(n_blocks):
            m = jnp.maximum(m, jnp.max(block_logits(s_ref, j) + bias[j], axis=1, keepdims=True))
        p_new = jnp.exp(s_new - m)
        l = jnp.sum(p_new, axis=1, keepdims=True)
        for j in range(n_blocks):
            pj = jnp.exp(block_logits(s_ref, j) + bias[j] - m)
            l = l + jnp.sum(pj, axis=1, keepdims=True)
            c, off = divmod(j, blocks_step)
            p_ref[c, :, off * MOBA_BLOCK:(off + 1) * MOBA_BLOCK] = pj.astype(BF16)
        l_ref[...] = jnp.broadcast_to(l, l_ref.shape)
        acc_ref[...] = _dot(p_new.astype(BF16), vn_ref[...].astype(BF16))

    @pl.when(u >= n_kc)
    def _():
        acc_ref[...] += _dot_nt(p_ref[u - n_kc], chunk_t)

    @pl.when(last_u)
    def _():
        o_ref[0] = acc_ref[...] / l_ref[:, 0:1]


def _paged_attn_call(page_table_flat, cache_k4, cache_v4, q, k_new, v_new, n_seq, pages_per_seq, t_new):
    _, heads, hd, page = cache_k4.shape
    width = heads * hd
    past = pages_per_seq * page
    n_blocks = past // MOBA_BLOCK
    pages_step = 16
    while pages_per_seq % pages_step:
        pages_step //= 2
    rows_step = pages_step * page
    n_cols = heads * t_new
    assert rows_step % MOBA_BLOCK == 0 and 2 * n_cols <= LANES and n_blocks <= LANES
    n_kc = pages_per_seq // pages_step
    n_sel = min(MOBA_TOPK, n_blocks)
    new_spec = pl.BlockSpec((t_new, width), lambda n, u, pt: (n, 0))
    grid_spec = pltpu.PrefetchScalarGridSpec(
        num_scalar_prefetch=1,
        grid=(n_seq, 2 * n_kc),
        in_specs=[pl.BlockSpec(memory_space=pl.ANY), pl.BlockSpec(memory_space=pl.ANY), new_spec, new_spec, new_spec],
        out_specs=pl.BlockSpec((1, n_cols, width), lambda n, u, pt: (n, 0, 0)),
        scratch_shapes=[
            pltpu.VMEM((2, pages_step, heads, hd, page), F32),
            pltpu.SemaphoreType.DMA((2,)),
            pltpu.VMEM((2 * n_cols, width), BF16),
            pltpu.VMEM((n_kc, n_cols, rows_step), F32),
            pltpu.VMEM((n_kc, n_cols, rows_step), BF16),
            pltpu.VMEM((n_cols, LANES), F32),
            pltpu.VMEM((n_cols, width), F32),
        ],
    )
    return pl.pallas_call(
        functools.partial(_paged_attn_kernel, pages_per_seq=pages_per_seq, n_blocks=n_blocks, n_sel=n_sel),
        grid_spec=grid_spec,
        out_shape=jax.ShapeDtypeStruct((n_seq, n_cols, width), F32),
        compiler_params=_cparams(("arbitrary", "arbitrary"), 40),
        name="sample_attention",
    )(page_table_flat, cache_k4, cache_v4, q, k_new, v_new)


def _rope_tables(pos):
    inv_freq = ROPE_THETA ** (-jnp.arange(0, A_HEAD_DIM, 2, dtype=F32) / A_HEAD_DIM)
    ang = pos.astype(F32)[:, None] * inv_freq[None, :]
    cos = jnp.concatenate([jnp.cos(ang)] * (2 * LANES // A_HEAD_DIM), axis=-1)
    sin = jnp.concatenate([jnp.sin(ang)] * (2 * LANES // A_HEAD_DIM), axis=-1)
    return cos, sin


def _dispatch_tables(idx, rank, counts, sub):
    k, m = idx.shape
    rows_g = MOE_GROUP_ROWS
    padded = (counts + rows_g - 1) // rows_g * rows_g
    pad_end = jnp.cumsum(padded)
    pad_start = pad_end - padded
    n_groups = -(-(k * m) // rows_g) + N_EXPERTS
    onehot = idx[:, :, None] == jnp.arange(N_EXPERTS, dtype=I32)[None, None, :]
    pos = jnp.sum(jnp.where(onehot, pad_start[None, None, :], 0), axis=-1) + rank
    group_start = jnp.arange(n_groups, dtype=I32) * rows_g
    group_expert = jnp.minimum(jnp.sum(pad_end[None, :] <= group_start[:, None], axis=1), N_EXPERTS - 1).astype(I32)
    n_used = (pad_end[-1:] // rows_g).astype(I32)
    pos_tiles = (pos * sub).reshape(k, m // ROW_TILE, ROW_TILE).transpose(1, 0, 2).reshape(-1).astype(I32)
    return group_expert, n_used, pad_start.astype(I32), pos_tiles, n_groups


def kernel(x_prompt, x_sample, cache_k, cache_v, state_gla, page_table, c_prompt, c_sample, w_ada, b_ada,
           norm_mix_w, norm_ffn_w, norm_final_w, w_in, w_gate_up, b_gate, gla_norm_w, w_proj_a, w_proj_b, w_out,
           w_router, b_router, w_e1, w_e3, w_e2, w_s1, w_s3, w_s2):
    batch, seq, d = x_prompt.shape
    n_dec, t_new, _ = x_sample.shape
    depth = w_ada.shape[0]
    n_pool, page = cache_k.shape[1], cache_k.shape[2]
    pages_per_seq = page_table.shape[1]
    past_len = pages_per_seq * page
    assert depth == 1 and past_len % MOBA_BLOCK == 0 and MOBA_BLOCK % page == 0
    assert seq % ROW_TILE == 0 and ROW_TILE % MOBA_BLOCK == 0 and ROW_TILE % t_new == 0 and t_new == SUBLANES
    assert seq % ATTN_KEY_CHUNK == 0 and ATTN_KEY_CHUNK % MOBA_BLOCK == 0 and (m_all := batch * seq + n_dec * t_new) % ROW_TILE == 0
    del m_all
    m_p, m_s = batch * seq, n_dec * t_new
    bkw = w_gate_up.shape[2]
    dk = bkw // B_HEADS
    dv = gla_norm_w.shape[1]
    bvw = B_HEADS * dv

    n_c = batch + n_dec
    n_c_pad = -(-n_c // SUBLANES) * SUBLANES
    c_all = jnp.concatenate([c_prompt, c_sample, jnp.zeros((n_c_pad - n_c, d), F32)], axis=0)
    mod = _ada_call(c_all, w_ada[0], b_ada[0])
    mod_p = mod[:batch].reshape(batch, 1, 6 * d)
    mod_s = mod[batch:n_c].reshape(n_dec, 1, 6 * d)

    sizes = (A_WIDTH, A_WIDTH, A_WIDTH, bkw, bkw, bvw, bvw, GATE_RANK, d, d)
    offs = [0]
    for s in sizes:
        offs.append(offs[-1] + s)
    w_in0 = w_in[0]
    w_main = jnp.concatenate([w_in0[:, :offs[7]], w_in0[:, offs[8]:]], axis=1).astype(BF16)
    w_gr = jnp.pad(w_in0[:, offs[7]:offs[8]], ((0, 0), (0, LANES - GATE_RANK))).astype(BF16)
    wg = jnp.pad(w_gate_up[0].reshape(GATE_RANK, B_HEADS, dk).transpose(1, 0, 2),
                 ((0, 0), (0, LANES - GATE_RANK), (0, 0)))
    bg = b_gate[0].reshape(B_HEADS, 1, dk)
    nw1 = norm_mix_w[0].reshape(1, d)
    nw2 = norm_ffn_w[0].reshape(1, d)
    nwf = norm_final_w.reshape(1, d)
    gnw = gla_norm_w[0].reshape(1, dv)
    wpa = w_proj_a[0].astype(BF16).reshape(A_HEADS, A_HEAD_DIM, d)
    wpb = w_proj_b[0].astype(BF16)
    wout = w_out[0].astype(BF16)
    wr_t = w_router[0].T
    br_lanes = jnp.broadcast_to(b_router[0][:, None], (N_EXPERTS, LANES))
    we1, we3, we2 = w_e1[0].astype(BF16), w_e3[0].astype(BF16), w_e2[0].astype(BF16)
    ws1, ws3, ws2 = w_s1[0].astype(BF16), w_s3[0].astype(BF16), w_s2[0].astype(BF16)

    tiles_per_seq = seq // ROW_TILE
    groups_s = ROW_TILE // t_new

    cos_p, sin_p = _rope_tables(jnp.arange(seq, dtype=I32))
    xp3 = x_prompt.reshape(m_p // ROW_TILE, ROW_TILE, d)
    (qa_p, ka_p, va_p, qb_p, kb_p, vb_p, gb_p, ga_p, gm_p, gr_p, kmean_p) = _inproj_call(
        xp3, mod_p, nw1, cos_p, sin_p, w_main, w_gr, groups_per_tile=1, rows_per_group=ROW_TILE,
        mod_of_tile=lambda i: i // tiles_per_seq, cos_of_tile=lambda i: i % tiles_per_seq, with_kmean=True,
        name="inproj_prompt")
    qaug, kaug, vaug = _select_call(qa_p, ka_p, va_p, kmean_p, batch, seq)
    oa_p = _moba_attn_call(qaug, kaug, vaug)
    ob_p, state_p = _gla_call(qb_p, kb_p, vb_p, gr_p, wg, bg, None, n_seq=batch, seq_len=seq, chunk=GLA_CHUNK,
                              chunks_per_step=GLA_CHUNKS_PER_STEP, name="gla_prompt")
    x1_p, h2_p, lg_p = _mixer_call(
        xp3, oa_p, ob_p, gb_p, ga_p, gm_p, mod_p, nw2, gnw, wpa, wpb, wout, wr_t, groups_per_tile=1,
        mod_of_tile=lambda i: i // tiles_per_seq,
        oa_of_tile=lambda i: (i // tiles_per_seq, 0, i % tiles_per_seq, 0), name="mixer_prompt")

    cos_s, sin_s = _rope_tables(past_len + jnp.arange(t_new, dtype=I32))
    cos_s = jnp.tile(cos_s, (groups_s, 1))
    sin_s = jnp.tile(sin_s, (groups_s, 1))
    (qa_s, ka_s, va_s, qb_s, kb_s, vb_s, gb_s, ga_s, gm_s, gr_s) = _inproj_call(
        x_sample, mod_s, nw1, cos_s, sin_s, w_main, w_gr, groups_per_tile=groups_s, rows_per_group=t_new,
        mod_of_tile=lambda i: i, cos_of_tile=lambda i: 0, with_kmean=False, name="inproj_sample")
    pt_flat = page_table.reshape(-1)
    ck4 = cache_k[0].transpose(0, 2, 3, 1)
    cv4 = cache_v[0].transpose(0, 2, 3, 1)
    o_c = _paged_attn_call(pt_flat, ck4, cv4, qa_s, ka_s, va_s, n_dec, pages_per_seq, t_new)
    o5 = o_c.reshape(n_dec, A_HEADS, t_new, A_HEADS, A_HEAD_DIM)
    hh = jnp.arange(A_HEADS)
    oa_s = o5[:, hh, :, hh, :].reshape(1, A_HEADS, m_s, A_HEAD_DIM)
    ob_s, state_s = _gla_call(qb_s, kb_s, vb_s, gr_s, wg, bg, state_gla[0], n_seq=n_dec, seq_len=t_new, chunk=t_new,
                              chunks_per_step=1, name="gla_sample")
    x1_s, h2_s, lg_s = _mixer_call(
        x_sample, oa_s, ob_s, gb_s, ga_s, gm_s, mod_s, nw2, gnw, wpa, wpb, wout, wr_t, groups_per_tile=groups_s,
        mod_of_tile=lambda i: i, oa_of_tile=lambda i: (0, 0, i, 0), name="mixer_sample")

    sub = d // LANES
    h2g = jnp.concatenate([h2_p, h2_s], axis=0)
    idx, gw, rank, cnt = _route_call(jnp.concatenate([lg_p, lg_s], axis=1), br_lanes)
    counts = cnt[:, 0]
    group_expert, n_used, pad_start, pos_tiles, n_groups = _dispatch_tables(idx, rank, counts, sub)
    xsorted = _dispatch_call(counts, pad_start, n_used, pos_tiles, h2g, n_groups * MOE_GROUP_ROWS, d)
    ysorted = _expert_call(group_expert, n_used, xsorted, we1, we3, we2)
    n_tiles_p = m_p // ROW_TILE
    y_p = _final_call(pos_tiles, gw, x1_p, h2g, mod_p, nwf, ws1, ws3, ws2, ysorted, groups_per_tile=1,
                      mod_of_tile=lambda i: i // tiles_per_seq, first_tile=0, name="final_prompt")
    y_s = _final_call(pos_tiles, gw, x1_s, h2g, mod_s, nwf, ws1, ws3, ws2, ysorted, groups_per_tile=groups_s,
                      mod_of_tile=lambda i: i, first_tile=n_tiles_p, name="final_sample")

    return (
        y_p.reshape(batch, seq, d),
        y_s,
        ka_p.reshape(1, batch, seq, A_HEADS, A_HEAD_DIM),
        va_p.reshape(1, batch, seq, A_HEADS, A_HEAD_DIM),
        state_p[None],
        ka_s.reshape(1, n_dec, t_new, A_HEADS, A_HEAD_DIM),
        va_s.reshape(1, n_dec, t_new, A_HEADS, A_HEAD_DIM),
        state_s[None],
    )
```

```python
import functools

import jax
import jax.numpy as jnp
from jax import lax
from jax.experimental import pallas as pl
from jax.experimental.pallas import tpu as pltpu

F32 = jnp.float32
BF16 = jnp.bfloat16
I32 = jnp.int32
HIGHEST = lax.Precision.HIGHEST

A_HEADS = 8
A_HEAD_DIM = 64
A_WIDTH = A_HEADS * A_HEAD_DIM
MOBA_BLOCK = 256
MOBA_TOPK = 3
ROPE_THETA = 10000.0
B_HEADS = 4
GATE_RANK = 16
GATE_NORMALIZER = 16.0
N_EXPERTS = 64
N_GROUPS = 8
GROUP_SIZE = N_EXPERTS // N_GROUPS
TOPK_GROUPS = 4
TOP_K = 8
ROUTED_SCALE = 2.5
EPS = 1e-6
NEG = -1e30
BELOW_NEG = -3e38

LANES = 128
SUBLANES = 8
VMEM_BYTES = 64 * 1024 * 1024

LOG2_E = 1.4426950408889634
ATTN_KEY_CHUNK = 2048
ATTN_HEADS_PER_STEP = 2

PAGED_ATTN_PAGES_PER_STEP = 32

ROW_TILE = 256
GLA_CHUNK = 64
GLA_CHUNKS_PER_STEP = 4
MOE_GROUP_ROWS = 256


def _cparams(semantics, vmem_mb):
    return pltpu.CompilerParams(dimension_semantics=semantics, vmem_limit_bytes=vmem_mb * 1024 * 1024)


def _silu(x):
    return x * jax.nn.sigmoid(x)


def _dot(a, b):
    return jnp.dot(a, b, preferred_element_type=F32)


def _dot_nt(a, b, precision=None):
    return lax.dot_general(a, b, (((1,), (1,)), ((), ())), precision=precision, preferred_element_type=F32)


def _dot_tn(a, b):
    return lax.dot_general(a, b, (((0,), (0,)), ((), ())), preferred_element_type=F32)


def _first_argmax(vals, iota, axis, size):
    mx = jnp.max(vals, axis=axis, keepdims=True)
    return jnp.min(jnp.where(vals == mx, iota, size), axis=axis, keepdims=True)


def _store_row_tiles(ref, x):
    rows, d = x.shape
    sub = d // LANES
    for s in range(sub):
        ref[pl.ds(s, rows, stride=sub), :] = x[:, s * LANES:(s + 1) * LANES]


def _load_row_tiles(ref, rows, sub):
    return jnp.concatenate([ref[pl.ds(s, rows, stride=sub), :] for s in range(sub)], axis=1)


def _ada_kernel(c_ref, w_ref, b_ref, o_ref):
    s = _silu(c_ref[...])
    o_ref[...] = jnp.dot(s, w_ref[...], precision=HIGHEST, preferred_element_type=F32) + b_ref[...]


def _ada_call(c_all, w_ada, b_ada):
    n, d = c_all.shape
    width = w_ada.shape[1]
    tn = width // 4
    return pl.pallas_call(
        _ada_kernel,
        grid=(width // tn,),
        in_specs=[
            pl.BlockSpec((n, d), lambda j: (0, 0)),
            pl.BlockSpec((d, tn), lambda j: (0, j)),
            pl.BlockSpec((1, tn), lambda j: (0, j)),
        ],
        out_specs=pl.BlockSpec((n, tn), lambda j: (0, j)),
        out_shape=jax.ShapeDtypeStruct((n, width), F32),
        compiler_params=_cparams(("arbitrary",), 40),
        name="adaln",
    )(c_all, w_ada, b_ada.reshape(1, width))


def _rope(a, cos, sin):
    lane = lax.broadcasted_iota(I32, cos.shape, 1)
    first_half = (lane % A_HEAD_DIM) < (A_HEAD_DIM // 2)
    outs = []
    for c in range(a.shape[1] // LANES):
        xc = a[:, c * LANES:(c + 1) * LANES]
        rot = jnp.where(first_half, -pltpu.roll(xc, LANES - A_HEAD_DIM // 2, 1), pltpu.roll(xc, A_HEAD_DIM // 2, 1))
        outs.append(xc * cos + rot * sin)
    return jnp.concatenate(outs, axis=1)


def _modulated_norm(x, norm_w, scale, shift):
    ms = jnp.mean(x * x, axis=-1, keepdims=True)
    y = x * lax.rsqrt(ms + EPS) * norm_w
    return y * (1.0 + scale) + shift


def _inproj_kernel(x_ref, sh_ref, sc_ref, nw_ref, cos_ref, sin_ref, w_ref, wgr_ref,
                   q_ref, k_ref, v_ref, qb_ref, kb_ref, vb_ref, gb_ref, ga_ref, gm_ref, gr_ref, *km_refs):
    g, r, d = x_ref.shape
    h = _modulated_norm(x_ref[...], nw_ref[...], sc_ref[...], sh_ref[...])
    hb = h.reshape(g * r, d).astype(BF16)
    cos = cos_ref[...]
    sin = sin_ref[...]
    col = 0

    def proj(width):
        nonlocal col
        out = _dot(hb, w_ref[:, col:col + width])
        col += width
        return out

    q_ref[...] = _rope(proj(A_WIDTH), cos, sin)
    k = _rope(proj(A_WIDTH), cos, sin)
    k_ref[...] = k
    if km_refs:
        nb = (g * r) // MOBA_BLOCK
        km_refs[0][...] = jnp.mean(k.reshape(nb, MOBA_BLOCK, A_WIDTH), axis=1, keepdims=True)
    v_ref[...] = proj(A_WIDTH)
    for ref in (qb_ref, kb_ref, vb_ref, gb_ref, ga_ref, gm_ref):
        ref[...] = proj(ref.shape[1])
    gr_ref[...] = _dot(hb, wgr_ref[...])


def _inproj_call(x3, mod3, norm_w, cos, sin, w_main, w_gr, *, groups_per_tile, rows_per_group, mod_of_tile,
                 cos_of_tile, with_kmean, name):
    n_groups, r, d = x3.shape
    assert r == rows_per_group
    m = n_groups * r
    tm = groups_per_tile * r
    n_tiles = n_groups // groups_per_tile
    d_model = d
    bk = w_main.shape[1] - 3 * A_WIDTH - 4 * d_model
    widths = [A_WIDTH, A_WIDTH, A_WIDTH, bk // 2, bk // 2, d_model, d_model, d_model, d_model, LANES]
    out_shape = [jax.ShapeDtypeStruct((m, w), F32) for w in widths]
    out_specs = [pl.BlockSpec((tm, w), lambda i: (i, 0)) for w in widths]
    if with_kmean:
        out_shape.append(jax.ShapeDtypeStruct((m // MOBA_BLOCK, 1, A_WIDTH), F32))
        out_specs.append(pl.BlockSpec((tm // MOBA_BLOCK, 1, A_WIDTH), lambda i: (i, 0, 0)))
    resident = dict(pipeline_mode=pl.Buffered(1))
    return pl.pallas_call(
        _inproj_kernel,
        grid=(n_tiles,),
        in_specs=[
            pl.BlockSpec((groups_per_tile, r, d), lambda i: (i, 0, 0)),
            pl.BlockSpec((groups_per_tile, 1, d), lambda i: (mod_of_tile(i), 0, 0)),
            pl.BlockSpec((groups_per_tile, 1, d), lambda i: (mod_of_tile(i), 0, 1)),
            pl.BlockSpec((1, d), lambda i: (0, 0), **resident),
            pl.BlockSpec((tm, LANES), lambda i: (cos_of_tile(i), 0)),
            pl.BlockSpec((tm, LANES), lambda i: (cos_of_tile(i), 0)),
            pl.BlockSpec(w_main.shape, lambda i: (0, 0), **resident),
            pl.BlockSpec(w_gr.shape, lambda i: (0, 0), **resident),
        ],
        out_specs=out_specs,
        out_shape=out_shape,
        compiler_params=_cparams(("parallel",), 48),
        name=name,
    )(x3, mod3, mod3, norm_w, cos, sin, w_main, w_gr)


def _select_kernel(q_ref, k_ref, v_ref, km_ref, qa_ref, ka_ref, va_ref, *, nbw):
    i = pl.program_id(1)
    q = q_ref[...]
    k = k_ref[...]
    v = v_ref[...]
    km = km_ref[:, 0, :]
    nblk = km.shape[0]
    rows = q.shape[0]
    q_t = q.T
    v_t = v.T
    blk = lax.broadcasted_iota(I32, (nblk, rows), 0)
    colw = lax.broadcasted_iota(I32, (rows, nbw), 1)
    ones_row = (lax.broadcasted_iota(I32, (A_HEAD_DIM, rows), 0) == 0).astype(F32)
    own_onehot = jnp.where(colw == i, -NEG, 0.0)
    for h in range(A_HEADS):
        sl = slice(h * A_HEAD_DIM, (h + 1) * A_HEAD_DIM)
        s = _dot_nt(km[:, sl], q[:, sl], precision=HIGHEST)
        s = jnp.where(blk < i, s, NEG)
        sel = jnp.zeros(s.shape, jnp.bool_)
        for _ in range(MOBA_TOPK):
            first = _first_argmax(s, blk, 0, nblk)
            hit = blk == first
            sel = sel | (hit & (first < i))
            s = jnp.where(hit, BELOW_NEG, s)
        selneg = jnp.where(sel, 0.0, -1.0)
        if nbw > nblk:
            selneg = jnp.concatenate([selneg, jnp.full((nbw - nblk, rows), -1.0, F32)], axis=0)
        qa_ref[0, h] = jnp.concatenate([q_t[sl, :] * (A_HEAD_DIM ** -0.5 * LOG2_E), selneg], axis=0).astype(BF16)
        ka_ref[0, h] = jnp.concatenate([k[:, sl], own_onehot], axis=1).astype(BF16)
        va_ref[0, h] = jnp.concatenate([v_t[sl, :], ones_row], axis=0).astype(BF16)


def _select_call(q, k, v, kmean, batch, seq):
    nblk = seq // MOBA_BLOCK
    nbw = -(-nblk // A_HEAD_DIM) * A_HEAD_DIM
    aw = A_HEAD_DIM + nbw
    row_spec = pl.BlockSpec((MOBA_BLOCK, A_WIDTH), lambda b, i: (b * nblk + i, 0))
    t_spec = lambda w: pl.BlockSpec((1, A_HEADS, w, MOBA_BLOCK), lambda b, i: (b, 0, 0, i))
    return pl.pallas_call(
        functools.partial(_select_kernel, nbw=nbw),
        grid=(batch, nblk),
        in_specs=[row_spec, row_spec, row_spec, pl.BlockSpec((nblk, 1, A_WIDTH), lambda b, i: (b, 0, 0))],
        out_specs=[t_spec(aw), pl.BlockSpec((1, A_HEADS, MOBA_BLOCK, aw), lambda b, i: (b, 0, i, 0)),
                   t_spec(2 * A_HEAD_DIM)],
        out_shape=[
            jax.ShapeDtypeStruct((batch, A_HEADS, aw, seq), BF16),
            jax.ShapeDtypeStruct((batch, A_HEADS, seq, aw), BF16),
            jax.ShapeDtypeStruct((batch, A_HEADS, 2 * A_HEAD_DIM, seq), BF16),
        ],
        compiler_params=_cparams(("parallel", "parallel"), 32),
        name="moba_select",
    )(q, k, v, kmean)


def _moba_attn_kernel(qa_ref, ka_ref, va_ref, kown_ref, vown_ref, o_ref, sa_ref, sb_ref, mxa_ref, mxb_ref,
                      m_ref, acc_ref):
    i = pl.program_id(2)
    chunk = ATTN_KEY_CHUNK
    heads = range(qa_ref.shape[1])

    def col_max(s_t):
        while s_t.shape[0] > SUBLANES:
            half = s_t.shape[0] // 2
            s_t = jnp.maximum(s_t[:half], s_t[half:])
        return s_t

    n_trips = (i * MOBA_BLOCK + chunk - 1) // chunk
    last_chunk = ka_ref.shape[2] // chunk - 1

    def stage_logits(buf, h, c):
        r0 = pl.multiple_of(jnp.minimum(c, last_chunk) * chunk, chunk)
        s_t = _dot(ka_ref[0, h, pl.ds(r0, chunk), :], qa_ref[0, h])
        buf[0][h] = s_t
        buf[1][h] = col_max(s_t)

    def consume(buf, h, c):
        r0 = pl.multiple_of(c * chunk, chunk)
        m_prev = m_ref[h, 0:1, :]
        m_new = jnp.maximum(m_prev, jnp.max(buf[1][h], axis=0, keepdims=True))
        alpha = jnp.exp2(m_prev - m_new)
        p_t = jnp.exp2(buf[0][h] - m_new).astype(BF16)
        acc_ref[h] = alpha * acc_ref[h] + _dot(va_ref[0, h, :, pl.ds(r0, chunk)], p_t)
        m_ref[h] = jnp.broadcast_to(m_new, m_ref.shape[1:])

    buf_a, buf_b = (sa_ref, mxa_ref), (sb_ref, mxb_ref)
    for h in heads:
        stage_logits(buf_a, h, 0)
    for h in heads:
        s_t = _dot(kown_ref[0, h][:, 0:A_HEAD_DIM], qa_ref[0, h][0:A_HEAD_DIM, :])
        key = lax.broadcasted_iota(I32, s_t.shape, 0)
        qry = lax.broadcasted_iota(I32, s_t.shape, 1)
        s_t = jnp.where(key <= qry, s_t, NEG)
        m_own = jnp.max(col_max(s_t), axis=0, keepdims=True)
        m_ref[h] = jnp.broadcast_to(m_own, m_ref.shape[1:])
        acc_ref[h] = _dot(vown_ref[0, h], jnp.exp2(s_t - m_own).astype(BF16))

    def past_chunk_pair(t, carry):
        c = 2 * t
        for h in heads:
            stage_logits(buf_b, h, c + 1)
        for h in heads:
            consume(buf_a, h, c)

        @pl.when(c + 1 < n_trips)
        def _():
            for h in heads:
                stage_logits(buf_a, h, c + 2)
            for h in heads:
                consume(buf_b, h, c + 1)

        return carry

    lax.fori_loop(0, (n_trips + 1) // 2, past_chunk_pair, 0)
    for h in heads:
        acc = acc_ref[h].T
        o_ref[0, h] = acc[:, :A_HEAD_DIM] / acc[:, A_HEAD_DIM:A_HEAD_DIM + 1]


def _moba_attn_call(qaug_t, kaug, vaug_t):
    batch, heads, seq, aw = kaug.shape
    nblk = seq // MOBA_BLOCK
    vw = vaug_t.shape[2]
    hs = ATTN_HEADS_PER_STEP
    return pl.pallas_call(
        _moba_attn_kernel,
        grid=(batch, heads // hs, nblk),
        in_specs=[
            pl.BlockSpec((1, hs, aw, MOBA_BLOCK), lambda b, h, i: (b, h, 0, i)),
            pl.BlockSpec((1, hs, seq, aw), lambda b, h, i: (b, h, 0, 0)),
            pl.BlockSpec((1, hs, vw, seq), lambda b, h, i: (b, h, 0, 0)),
            pl.BlockSpec((1, hs, MOBA_BLOCK, aw), lambda b, h, i: (b, h, i, 0)),
            pl.BlockSpec((1, hs, vw, MOBA_BLOCK), lambda b, h, i: (b, h, 0, i)),
        ],
        out_specs=pl.BlockSpec((1, hs, MOBA_BLOCK, A_HEAD_DIM), lambda b, h, i: (b, h, i, 0)),
        out_shape=jax.ShapeDtypeStruct((batch, heads, seq, A_HEAD_DIM), F32),
        scratch_shapes=[pltpu.VMEM((hs, ATTN_KEY_CHUNK, MOBA_BLOCK), F32),
                        pltpu.VMEM((hs, ATTN_KEY_CHUNK, MOBA_BLOCK), F32),
                        pltpu.VMEM((hs, SUBLANES, MOBA_BLOCK), F32),
                        pltpu.VMEM((hs, SUBLANES, MOBA_BLOCK), F32),
                        pltpu.VMEM((hs, SUBLANES, MOBA_BLOCK), F32), pltpu.VMEM((hs, vw, MOBA_BLOCK), F32)],
        compiler_params=_cparams(("parallel", "parallel", "arbitrary"), 56),
        name="moba_attention",
    )(qaug_t, kaug, vaug_t, kaug, vaug_t)


def _gla_level_masks(c, dk):
    row = lax.broadcasted_iota(I32, (c, c), 0)
    col = lax.broadcasted_iota(I32, (c, c), 1)
    r1 = lax.broadcasted_iota(I32, (c, dk), 0)
    levels = []
    half = c // 2
    while half >= 1:
        grp = 2 * half
        valid = (row // grp == col // grp) & (row % grp >= half) & (col % grp < half)
        picks = {} if grp >= SUBLANES else {
            off: r1 % grp == half - 1 + off for off in range(-(half - 1), half + 1) if off != 0}
        levels.append((half, valid, picks))
        half //= 2
    return row == col, levels


def _gla_intra_scores(q, k, b, masks):
    c, dk = q.shape
    diagonal, levels = masks
    scores = jnp.where(diagonal, _dot_nt(q.astype(BF16), k.astype(BF16)), 0.0)
    for half, valid, picks in levels:
        grp = 2 * half
        if grp >= SUBLANES:
            a = jnp.concatenate([jnp.broadcast_to(b[j * grp + half - 1:j * grp + half, :], (grp, dk))
                                 for j in range(c // grp)], axis=0)
        else:
            a = b
            for off, pick in picks.items():
                a = jnp.where(pick, pltpu.roll(b, off % c, 0), a)
        qf = (q * jnp.exp(jnp.minimum(b - a, 0.0))).astype(BF16)
        kf = (k * jnp.exp(jnp.minimum(a - b, 0.0))).astype(BF16)
        scores = jnp.where(valid, _dot_nt(qf, kf), scores)
    return scores


def _gla_kernel(*refs, chunk, n_chunks, has_s0):
    if has_s0:
        q_ref, k_ref, v_ref, gr_ref, wg_ref, bg_ref, s0_ref, o_ref, sout_ref, st_ref = refs
    else:
        q_ref, k_ref, v_ref, gr_ref, wg_ref, bg_ref, o_ref, sout_ref, st_ref = refs
    step = pl.program_id(1)
    heads, dv, dk = st_ref.shape

    @pl.when(step == 0)
    def _():
        for h in range(heads):
            if has_s0:
                st_ref[h] = s0_ref[0, h].T
            else:
                st_ref[h] = jnp.zeros((dv, dk), F32)

    tri = (lax.broadcasted_iota(I32, (chunk, chunk), 1) <= lax.broadcasted_iota(I32, (chunk, chunk), 0)).astype(F32)
    masks = _gla_level_masks(chunk, dk)
    for c in range(n_chunks):
        sl = slice(c * chunk, (c + 1) * chunk)
        z = jnp.dot(gr_ref[sl, :], wg_ref[...], precision=HIGHEST, preferred_element_type=F32) + bg_ref[...]
        log_a = -(jnp.maximum(-z, 0.0) + jnp.log1p(jnp.exp(-jnp.abs(z)))) * (1.0 / GATE_NORMALIZER)
        b_all = jnp.dot(tri, log_a, precision=HIGHEST, preferred_element_type=F32)
        for h in range(heads):
            kl = slice(h * dk, (h + 1) * dk)
            vl = slice(h * dv, (h + 1) * dv)
            q = q_ref[sl, kl] * (dk ** -0.5)
            k = k_ref[sl, kl]
            v = v_ref[sl, vl].astype(BF16)
            b = b_all[:, kl]
            b_last = b[chunk - 1:chunk, :]
            st = st_ref[h]
            scores = _gla_intra_scores(q, k, b, masks)
            inter = _dot_nt((q * jnp.exp(b)).astype(BF16), st.astype(BF16))
            o_ref[sl, vl] = _dot(scores.astype(BF16), v) + inter
            kd = (k * jnp.exp(b_last - b)).astype(BF16)
            st_ref[h] = st * jnp.exp(b_last) + _dot_tn(v, kd)

    @pl.when(step == pl.num_programs(1) - 1)
    def _():
        for h in range(heads):
            sout_ref[0, h] = st_ref[h].T


def _gla_call(qb, kb, vb, gr, wg, bg, s0, *, n_seq, seq_len, chunk, chunks_per_step, name):
    heads = B_HEADS
    kw, vw = qb.shape[1], vb.shape[1]
    dk, dv = kw // heads, vw // heads
    rb = chunk * chunks_per_step
    steps = seq_len // rb
    rows = lambda w: pl.BlockSpec((rb, w), lambda n, s: (n * steps + s, 0))
    state = pl.BlockSpec((1, heads, dk, dv), lambda n, s: (n, 0, 0, 0))
    in_specs = [rows(kw), rows(kw), rows(vw), rows(LANES),
                pl.BlockSpec(wg.shape, lambda n, s: (0, 0)), pl.BlockSpec(bg.shape, lambda n, s: (0, 0))]
    args = [qb, kb, vb, gr, wg, bg]
    if s0 is not None:
        in_specs.append(state)
        args.append(s0)
    return pl.pallas_call(
        functools.partial(_gla_kernel, chunk=chunk, n_chunks=chunks_per_step, has_s0=s0 is not None),
        grid=(n_seq, steps),
        in_specs=in_specs,
        out_specs=[rows(vw), state],
        out_shape=[
            jax.ShapeDtypeStruct((n_seq * seq_len, vw), F32),
            jax.ShapeDtypeStruct((n_seq, heads, dk, dv), F32),
        ],
        scratch_shapes=[pltpu.VMEM((heads, dv, dk), F32)],
        compiler_params=_cparams(("parallel", "arbitrary"), 40),
        name=name,
    )(*args)


def _mixer_kernel(x_ref, oa_ref, ob_ref, gb_ref, ga_ref, gm_ref, g1_ref, sh2_ref, sc2_ref, nw2_ref, gnw_ref,
                  wpa_ref, wpb_ref, wout_ref, wr_ref, x1_ref, h2_ref, lg_ref):
    g, r, d = x_ref.shape
    tm = g * r
    dv = gnw_ref.shape[1]
    ob = ob_ref[...]
    gb = gb_ref[...]
    gnw = gnw_ref[...]
    parts = []
    for h in range(ob.shape[1] // dv):
        oh = ob[:, h * dv:(h + 1) * dv]
        ms = jnp.mean(oh * oh, axis=-1, keepdims=True)
        parts.append((oh * lax.rsqrt(ms + EPS) * gnw * _silu(gb[:, h * dv:(h + 1) * dv])).astype(BF16))
    pb = _dot(jnp.concatenate(parts, axis=1), wpb_ref[...])
    pa = jnp.zeros((tm, d), F32)
    for h in range(A_HEADS):
        pa = pa + _dot(oa_ref[0, h].astype(BF16), wpa_ref[h])
    merged = jax.nn.sigmoid(ga_ref[...]) * pa + jax.nn.sigmoid(gm_ref[...]) * pb
    u = _dot(merged.astype(BF16), wout_ref[...])
    x1 = x_ref[...] + g1_ref[...] * u.reshape(g, r, d)
    x1_ref[...] = x1
    h2 = _modulated_norm(x1, nw2_ref[...], sc2_ref[...], sh2_ref[...]).reshape(tm, d)
    _store_row_tiles(h2_ref, h2)
    lg_ref[...] = _dot_nt(wr_ref[...], h2, precision=HIGHEST)


def _mixer_call(x3, oa4, ob, gb, ga, gm, mod3, nw2, gnw, wpa, wpb, wout, wr_t, *, groups_per_tile, mod_of_tile,
                oa_of_tile, name):
    n_groups, r, d = x3.shape
    m = n_groups * r
    tm = groups_per_tile * r
    n_tiles = n_groups // groups_per_tile
    row = lambda w: pl.BlockSpec((tm, w), lambda i: (i, 0))
    modspec = lambda k: pl.BlockSpec((groups_per_tile, 1, d), lambda i: (mod_of_tile(i), 0, k))
    const = lambda a: pl.BlockSpec(a.shape, lambda i: (0,) * a.ndim, pipeline_mode=pl.Buffered(1))
    return pl.pallas_call(
        _mixer_kernel,
        grid=(n_tiles,),
        in_specs=[
            pl.BlockSpec((groups_per_tile, r, d), lambda i: (i, 0, 0)),
            pl.BlockSpec((1, A_HEADS, tm, A_HEAD_DIM), lambda i: oa_of_tile(i)),
            row(ob.shape[1]), row(gb.shape[1]), row(d), row(d),
            modspec(2), modspec(3), modspec(4),
            const(nw2), const(gnw), const(wpa), const(wpb), const(wout), const(wr_t),
        ],
        out_specs=[
            pl.BlockSpec((groups_per_tile, r, d), lambda i: (i, 0, 0)),
            pl.BlockSpec((tm * d // LANES, LANES), lambda i: (i, 0)),
            pl.BlockSpec((N_EXPERTS, tm), lambda i: (0, i)),
        ],
        out_shape=[
            jax.ShapeDtypeStruct((n_groups, r, d), F32),
            jax.ShapeDtypeStruct((m * d // LANES, LANES), F32),
            jax.ShapeDtypeStruct((N_EXPERTS, m), F32),
        ],
        compiler_params=_cparams(("parallel",), 48),
        name=name,
    )(x3, oa4, ob, gb, ga, gm, mod3, mod3, mod3, nw2, gnw, wpa, wpb, wout, wr_t)


def _route_kernel(lg_ref, br_ref, idx_ref, gw_ref, rank_ref, cnt_ref, carry_ref):
    @pl.when(pl.program_id(0) == 0)
    def _():
        carry_ref[...] = jnp.zeros(carry_ref.shape, F32)

    logits = lg_ref[...]
    n_e, tm = logits.shape
    scores = jax.nn.sigmoid(logits)
    biased = scores + jnp.concatenate([br_ref[...]] * (tm // LANES), axis=1)
    g3 = biased.reshape(N_GROUPS, GROUP_SIZE, tm)
    sub = lax.broadcasted_iota(I32, g3.shape, 1)
    m1 = jnp.max(g3, axis=1, keepdims=True)
    f1 = jnp.min(jnp.where(g3 == m1, sub, GROUP_SIZE), axis=1, keepdims=True)
    m2 = jnp.max(jnp.where(sub == f1, -jnp.inf, g3), axis=1, keepdims=True)
    gs = (m1 + m2).reshape(N_GROUPS, tm)
    giota = lax.broadcasted_iota(I32, gs.shape, 0)
    gmask = jnp.zeros(gs.shape, jnp.bool_)
    for _ in range(TOPK_GROUPS):
        first = _first_argmax(gs, giota, 0, N_GROUPS)
        hit = giota == first
        gmask = gmask | hit
        gs = jnp.where(hit, -jnp.inf, gs)
    emask = jnp.broadcast_to(gmask.reshape(N_GROUPS, 1, tm), (N_GROUPS, GROUP_SIZE, tm)).reshape(n_e, tm)
    masked = jnp.where(emask, biased, -jnp.inf)
    eiota = lax.broadcasted_iota(I32, masked.shape, 0)
    idx_rows, w_rows = [], []
    for _ in range(TOP_K):
        first = _first_argmax(masked, eiota, 0, n_e)
        hit = eiota == first
        idx_rows.append(first)
        w_rows.append(jnp.sum(jnp.where(hit, scores, 0.0), axis=0, keepdims=True))
        masked = jnp.where(hit, -jnp.inf, masked)
    w = jnp.concatenate(w_rows, axis=0)
    idx_ref[...] = jnp.concatenate(idx_rows, axis=0)
    gw_ref[...] = w / jnp.sum(w, axis=0, keepdims=True) * ROUTED_SCALE
    chosen = jnp.zeros(masked.shape, F32)
    for first in idx_rows:
        chosen = chosen + (eiota == first).astype(F32)
    earlier = (lax.broadcasted_iota(I32, (tm, tm), 0) < lax.broadcasted_iota(I32, (tm, tm), 1)).astype(BF16)
    carry = carry_ref[...]
    base = _dot(chosen.astype(BF16), earlier) + jnp.concatenate([carry] * (tm // LANES), axis=1)
    rank_ref[...] = jnp.concatenate(
        [jnp.sum(jnp.where(eiota == first, base, 0.0), axis=0, keepdims=True) for first in idx_rows], axis=0).astype(I32)
    carry = carry + jnp.sum(chosen, axis=1, keepdims=True)
    carry_ref[...] = carry
    cnt_ref[...] = carry.astype(I32)


def _route_call(logits_t, b_router_lanes):
    n_e, m = logits_t.shape
    tm = 512 if m % 512 == 0 else ROW_TILE
    tok = lambda dt: jax.ShapeDtypeStruct((TOP_K, m), dt)
    tok_spec = pl.BlockSpec((TOP_K, tm), lambda i: (0, i))
    return pl.pallas_call(
        _route_kernel,
        grid=(m // tm,),
        in_specs=[pl.BlockSpec((n_e, tm), lambda i: (0, i)), pl.BlockSpec((n_e, LANES), lambda i: (0, 0))],
        out_specs=[tok_spec, tok_spec, tok_spec, pl.BlockSpec((n_e, LANES), lambda i: (0, 0))],
        out_shape=[tok(I32), tok(F32), tok(I32), jax.ShapeDtypeStruct((n_e, LANES), I32)],
        scratch_shapes=[pltpu.VMEM((n_e, LANES), F32)],
        compiler_params=_cparams(("arbitrary",), 32),
        name="moe_route",
    )(logits_t, b_router_lanes)


DMA_ISSUE_UNROLL = 8


def _dispatch_kernel(cnt_ref, pstart_ref, nu_ref, pos_ref, h2g_ref, xs_hbm, zero_ref, sem, zsem, gsem, *, tm, sub):
    i = pl.program_id(0)
    group_sub = MOE_GROUP_ROWS * sub

    @pl.when(i == 0)
    def _():
        zero_ref[...] = jnp.zeros(zero_ref.shape, F32)
        zero_row = zero_ref.at[pl.ds(0, sub), :]

        def per_expert(e, n_zero):
            count = cnt_ref[e]
            padded = (count + MOE_GROUP_ROWS - 1) // MOE_GROUP_ROWS * MOE_GROUP_ROWS
            base = pstart_ref[e]

            def fill(r, carry):
                dst = xs_hbm.at[pl.ds(pl.multiple_of((base + r) * sub, sub), sub), :]
                pltpu.make_async_copy(zero_row, dst, zsem).start()
                return carry

            lax.fori_loop(count, padded, fill, 0)
            return n_zero + (padded - count)

        n_zero = lax.fori_loop(0, N_EXPERTS, per_expert, 0)
        n_groups = xs_hbm.shape[0] // group_sub

        def fill_group(g, carry):
            dst = xs_hbm.at[pl.ds(pl.multiple_of(g * group_sub, group_sub), group_sub), :]
            pltpu.make_async_copy(zero_ref, dst, gsem).start()
            return carry

        lax.fori_loop(nu_ref[0], n_groups, fill_group, 0)

        def drain(r, carry):
            pltpu.make_async_copy(zero_row, xs_hbm.at[pl.ds(0, sub), :], zsem).wait()
            return carry

        lax.fori_loop(0, n_zero, drain, 0)

        def drain_group(g, carry):
            pltpu.make_async_copy(zero_ref, xs_hbm.at[pl.ds(0, group_sub), :], gsem).wait()
            return carry

        lax.fori_loop(nu_ref[0], n_groups, drain_group, 0)

    def issue(tb, carry):
        for u in range(DMA_ISSUE_UNROLL):
            t = tb * DMA_ISSUE_UNROLL + u
            src = h2g_ref.at[pl.ds(pl.multiple_of(t * sub, sub), sub), :]
            for k in range(TOP_K):
                p = pl.multiple_of(pos_ref[k * tm + t], sub)
                pltpu.make_async_copy(src, xs_hbm.at[pl.ds(p, sub), :], sem).start()
        return carry

    lax.fori_loop(0, tm // DMA_ISSUE_UNROLL, issue, 0)
    for _ in range(TOP_K):
        pltpu.make_async_copy(h2g_ref, xs_hbm.at[pl.ds(0, tm * sub), :], sem).wait()


def _dispatch_call(counts, pad_start, n_used, pos_tiles, h2g, n_rows_sorted, d):
    sub = d // LANES
    tm = ROW_TILE
    m = h2g.shape[0] // sub
    grid_spec = pltpu.PrefetchScalarGridSpec(
        num_scalar_prefetch=3,
        grid=(m // tm,),
        in_specs=[
            pl.BlockSpec((TOP_K * tm,), lambda i, c, p, u: (i,), memory_space=pltpu.SMEM),
            pl.BlockSpec((tm * sub, LANES), lambda i, c, p, u: (i, 0)),
        ],
        out_specs=pl.BlockSpec(memory_space=pl.ANY),
        scratch_shapes=[pltpu.VMEM((MOE_GROUP_ROWS * sub, LANES), F32), pltpu.SemaphoreType.DMA(()),
                        pltpu.SemaphoreType.DMA(()), pltpu.SemaphoreType.DMA(())],
    )
    return pl.pallas_call(
        functools.partial(_dispatch_kernel, tm=tm, sub=sub),
        grid_spec=grid_spec,
        out_shape=jax.ShapeDtypeStruct((n_rows_sorted * sub, LANES), F32),
        compiler_params=_cparams(("arbitrary",), 32),
        name="moe_dispatch",
    )(counts, pad_start, n_used, pos_tiles, h2g)


def _expert_kernel(ge_ref, nu_ref, x_ref, w1_ref, w3_ref, w2_ref, y_ref):
    g = pl.program_id(0)
    rows = MOE_GROUP_ROWS
    sub = x_ref.shape[0] // rows

    @pl.when(g < nu_ref[0])
    def _():
        xb = _load_row_tiles(x_ref, rows, sub).astype(BF16)
        hidden = _silu(_dot(xb, w1_ref[0])) * _dot(xb, w3_ref[0])
        _store_row_tiles(y_ref, _dot(hidden.astype(BF16), w2_ref[0]))

    @pl.when(g >= nu_ref[0])
    def _():
        y_ref[...] = jnp.zeros(y_ref.shape, F32)


def _expert_call(group_expert, n_used, xs, w1, w3, w2):
    n_groups = group_expert.shape[0]
    rows = MOE_GROUP_ROWS
    _, d, hidden = w1.shape
    sub = d // LANES
    used = lambda g, nu: jnp.minimum(g, nu[0] - 1)
    grid_spec = pltpu.PrefetchScalarGridSpec(
        num_scalar_prefetch=2,
        grid=(n_groups,),
        in_specs=[
            pl.BlockSpec((rows * sub, LANES), lambda g, ge, nu: (used(g, nu), 0)),
            pl.BlockSpec((1, d, hidden), lambda g, ge, nu: (ge[used(g, nu)], 0, 0)),
            pl.BlockSpec((1, d, hidden), lambda g, ge, nu: (ge[used(g, nu)], 0, 0)),
            pl.BlockSpec((1, hidden, d), lambda g, ge, nu: (ge[used(g, nu)], 0, 0)),
        ],
        out_specs=pl.BlockSpec((rows * sub, LANES), lambda g, ge, nu: (g, 0)),
    )
    return pl.pallas_call(
        _expert_kernel,
        grid_spec=grid_spec,
        out_shape=jax.ShapeDtypeStruct((n_groups * rows * sub, LANES), F32),
        compiler_params=_cparams(("arbitrary",), 40),
        name="moe_experts",
    )(group_expert, n_used, xs, w1, w3, w2)


def _start_tile_gather(idx_ref, n_rows, sub, src_hbm, dst, sem):
    def issue(rb, carry):
        for u in range(DMA_ISSUE_UNROLL):
            r = rb * DMA_ISSUE_UNROLL + u
            p = pl.multiple_of(idx_ref[r], sub)
            pltpu.make_async_copy(src_hbm.at[pl.ds(p, sub), :],
                                  dst.at[pl.ds(pl.multiple_of(r * sub, sub), sub), :], sem).start()
        return carry

    lax.fori_loop(0, n_rows // DMA_ISSUE_UNROLL, issue, 0)


def _final_kernel(pos_ref, pos_next_ref, gw_ref, x1_ref, h2g_ref, g2_ref, nwf_ref, ws1_ref, ws3_ref, ws2_ref, ys_hbm,
                  y_ref, buf, sem):
    i = pl.program_id(0)
    n_i = pl.num_programs(0)
    g, r, d = x1_ref.shape
    tm = g * r
    sub = d // LANES
    n_rows = TOP_K * tm
    slot = i % 2

    @pl.when(i == 0)
    def _():
        _start_tile_gather(pos_ref, n_rows, sub, ys_hbm, buf.at[0], sem.at[0])

    @pl.when(i + 1 < n_i)
    def _():
        _start_tile_gather(pos_next_ref, n_rows, sub, ys_hbm, buf.at[1 - slot], sem.at[1 - slot])

    hb = _load_row_tiles(h2g_ref, tm, sub).astype(BF16)
    shared = _dot((_silu(_dot(hb, ws1_ref[...])) * _dot(hb, ws3_ref[...])).astype(BF16), ws2_ref[...])
    gw_t = jnp.concatenate([gw_ref[...], jnp.zeros((LANES - TOP_K, tm), F32)], axis=0).T
    gates = [jnp.broadcast_to(gw_t[:, k:k + 1], (tm, LANES)) for k in range(TOP_K)]
    pltpu.make_async_copy(ys_hbm.at[pl.ds(0, n_rows * sub), :], buf.at[slot], sem.at[slot]).wait()
    parts = []
    for s in range(sub):
        acc = None
        for k in range(TOP_K):
            term = buf[slot, pl.ds(k * tm * sub + s, tm, stride=sub), :] * gates[k]
            acc = term if acc is None else acc + term
        parts.append(acc)
    routed = jnp.concatenate(parts, axis=1)
    x2 = x1_ref[...] + g2_ref[...] * (shared + routed).reshape(g, r, d)
    ms = jnp.mean(x2 * x2, axis=-1, keepdims=True)
    y_ref[...] = x2 * lax.rsqrt(ms + EPS) * nwf_ref[...]


def _final_call(pos_tiles, gw, x3, h2g, mod3, nwf, ws1, ws3, ws2, ysorted, *, groups_per_tile, mod_of_tile,
                first_tile, name):
    n_groups, r, d = x3.shape
    tm = groups_per_tile * r
    sub = d // LANES
    n_tiles = n_groups // groups_per_tile
    n_rows = TOP_K * tm
    const = lambda a: pl.BlockSpec(a.shape, lambda i: (0,) * a.ndim, pipeline_mode=pl.Buffered(1))
    return pl.pallas_call(
        _final_kernel,
        grid=(n_tiles,),
        in_specs=[
            pl.BlockSpec((n_rows,), lambda i: (first_tile + i,), memory_space=pltpu.SMEM),
            pl.BlockSpec((n_rows,), lambda i: (first_tile + jnp.minimum(i + 1, n_tiles - 1),),
                         memory_space=pltpu.SMEM),
            pl.BlockSpec((TOP_K, tm), lambda i: (0, first_tile + i)),
            pl.BlockSpec((groups_per_tile, r, d), lambda i: (i, 0, 0)),
            pl.BlockSpec((tm * sub, LANES), lambda i: (first_tile + i, 0)),
            pl.BlockSpec((groups_per_tile, 1, d), lambda i: (mod_of_tile(i), 0, 5)),
            const(nwf), const(ws1), const(ws3), const(ws2),
            pl.BlockSpec(memory_space=pl.ANY),
        ],
        out_specs=pl.BlockSpec((groups_per_tile, r, d), lambda i: (i, 0, 0)),
        out_shape=jax.ShapeDtypeStruct((n_groups, r, d), F32),
        scratch_shapes=[pltpu.VMEM((2, n_rows * sub, LANES), F32), pltpu.SemaphoreType.DMA((2,))],
        compiler_params=_cparams(("arbitrary",), 56),
        name=name,
    )(pos_tiles, pos_tiles, gw, x3, h2g, mod3, nwf, ws1, ws3, ws2, ysorted)


def _paged_attn_kernel(pt_ref, ck_hbm, cv_hbm, q_ref, kn_ref, vn_ref, o_ref,
                       cbuf, sem, qbd_ref, s_ref, p_ref, l_ref, acc_ref, *, pages_per_seq, n_blocks, n_sel):
    n = pl.program_id(0)
    u = pl.program_id(1)
    n_seq = pl.num_programs(0)
    n_u = pl.num_programs(1)
    n_kc = n_u // 2
    _, pages_step, heads, hd, page = cbuf.shape
    width = heads * hd
    rows_step = pages_step * page
    blocks_step = rows_step // MOBA_BLOCK
    t_new = q_ref.shape[0]
    n_cols = heads * t_new
    slot = (n * n_u + u) % 2

    def start_chunk(nn, uu, s):
        def copy_pages(src_hbm, cc):
            for p in range(pages_step):
                page_id = pt_ref[nn * pages_per_seq + cc * pages_step + p]
                pltpu.make_async_copy(src_hbm.at[page_id], cbuf.at[s, p], sem.at[s]).start()

        @pl.when(uu < n_kc)
        def _():
            copy_pages(ck_hbm, uu)

        @pl.when(uu >= n_kc)
        def _():
            copy_pages(cv_hbm, uu - n_kc)

    @pl.when((n == 0) & (u == 0))
    def _():
        start_chunk(n, u, slot)

    last_u = u == n_u - 1

    @pl.when(jnp.logical_not(last_u & (n == n_seq - 1)))
    def _():
        start_chunk(jnp.where(last_u, n + 1, n), jnp.where(last_u, 0, u + 1), 1 - slot)

    @pl.when(u == 0)
    def _():
        q = q_ref[...] * (A_HEAD_DIM ** -0.5)
        qt = jnp.concatenate([q] * heads, axis=0)
        r_i = lax.broadcasted_iota(I32, qt.shape, 0)
        c_i = lax.broadcasted_iota(I32, qt.shape, 1)
        qbd = jnp.where(c_i // hd == r_i // t_new, qt, 0.0)
        hi = qbd.astype(BF16)
        qbd_ref[...] = jnp.concatenate([hi, (qbd - hi.astype(F32)).astype(BF16)], axis=0)

    def logits(k_t):
        s2 = _dot(qbd_ref[...], k_t)
        return s2[0:n_cols] + s2[n_cols:2 * n_cols]

    pltpu.make_async_copy(ck_hbm.at[pl.ds(0, pages_step)], cbuf.at[slot], sem.at[slot]).wait()
    chunk_t = jnp.concatenate([cbuf[slot, p].reshape(width, page) for p in range(pages_step)], axis=1).astype(BF16)

    @pl.when(u < n_kc)
    def _():
        s_ref[u] = logits(chunk_t)

    def block_logits(ref, j):
        c, off = divmod(j, blocks_step)
        return ref[c, :, off * MOBA_BLOCK:(off + 1) * MOBA_BLOCK]

    @pl.when(u == n_kc - 1)
    def _():
        lane = lax.broadcasted_iota(I32, (n_cols, LANES), 1)
        sc = jnp.full((n_cols, LANES), NEG, F32)
        for j in range(n_blocks):
            sc = jnp.where(lane == j, jnp.sum(block_logits(s_ref, j), axis=1, keepdims=True), sc)
        sel = jnp.zeros(sc.shape, jnp.bool_)
        for _ in range(n_sel):
            first = _first_argmax(sc, lane, 1, LANES)
            hit = lane == first
            sel = sel | hit
            sc = jnp.where(hit, BELOW_NEG, sc)
        keep = jnp.where(sel, 0.0, NEG)
        bias = [jnp.max(jnp.where(lane == j, keep, NEG), axis=1, keepdims=True) for j in range(n_blocks)]
        s_new = lax.dot_general(qbd_ref[...], kn_ref[...].astype(BF16), (((1,), (1,)), ((), ())),
                                preferred_element_type=F32)
        s_new = s_new[0:n_cols] + s_new[n_cols:2 * n_cols]
        r_i = lax.broadcasted_iota(I32, s_new.shape, 0)
        c_i = lax.broadcasted_iota(I32, s_new.shape, 1)
        s_new = jnp.where(c_i <= r_i % t_new, s_new, NEG)
        m = jnp.max(s_new, axis=1, keepdims=True)
        for j in range(n_blocks):
            m = jnp.maximum(m, jnp.max(block_logits(s_ref, j) + bias[j], axis=1, keepdims=True))
        p_new = jnp.exp(s_new - m)
        l = jnp.sum(p_new, axis=1, keepdims=True)
        for j in range(n_blocks):
            pj = jnp.exp(block_logits(s_ref, j) + bias[j] - m)
            l = l + jnp.sum(pj, axis=1, keepdims=True)
            c, off = divmod(j, blocks_step)
            p_ref[c, :, off * MOBA_BLOCK:(off + 1) * MOBA_BLOCK] = pj.astype(BF16)
        l_ref[...] = jnp.broadcast_to(l, l_ref.shape)
        acc_ref[...] = _dot(p_new.astype(BF16), vn_ref[...].astype(BF16))

    @pl.when(u >= n_kc)
    def _():
        acc_ref[...] += _dot_nt(p_ref[u - n_kc], chunk_t)

    @pl.when(last_u)
    def _():
        o_ref[0] = acc_ref[...] / l_ref[:, 0:1]


def _paged_attn_call(page_table_flat, cache_k4, cache_v4, q, k_new, v_new, n_seq, pages_per_seq, t_new):
    _, heads, hd, page = cache_k4.shape
    width = heads * hd
    past = pages_per_seq * page
    n_blocks = past // MOBA_BLOCK
    pages_step = PAGED_ATTN_PAGES_PER_STEP
    while pages_per_seq % pages_step:
        pages_step //= 2
    rows_step = pages_step * page
    n_cols = heads * t_new
    assert rows_step % MOBA_BLOCK == 0 and 2 * n_cols <= LANES and n_blocks <= LANES
    n_kc = pages_per_seq // pages_step
    n_sel = min(MOBA_TOPK, n_blocks)
    new_spec = pl.BlockSpec((t_new, width), lambda n, u, pt: (n, 0))
    grid_spec = pltpu.PrefetchScalarGridSpec(
        num_scalar_prefetch=1,
        grid=(n_seq, 2 * n_kc),
        in_specs=[pl.BlockSpec(memory_space=pl.ANY), pl.BlockSpec(memory_space=pl.ANY), new_spec, new_spec, new_spec],
        out_specs=pl.BlockSpec((1, n_cols, width), lambda n, u, pt: (n, 0, 0)),
        scratch_shapes=[
            pltpu.VMEM((2, pages_step, heads, hd, page), F32),
            pltpu.SemaphoreType.DMA((2,)),
            pltpu.VMEM((2 * n_cols, width), BF16),
            pltpu.VMEM((n_kc, n_cols, rows_step), F32),
            pltpu.VMEM((n_kc, n_cols, rows_step), BF16),
            pltpu.VMEM((n_cols, LANES), F32),
            pltpu.VMEM((n_cols, width), F32),
        ],
    )
    return pl.pallas_call(
        functools.partial(_paged_attn_kernel, pages_per_seq=pages_per_seq, n_blocks=n_blocks, n_sel=n_sel),
        grid_spec=grid_spec,
        out_shape=jax.ShapeDtypeStruct((n_seq, n_cols, width), F32),
        compiler_params=_cparams(("arbitrary", "arbitrary"), 56),
        name="sample_attention",
    )(page_table_flat, cache_k4, cache_v4, q, k_new, v_new)


def _rope_tables(pos):
    inv_freq = ROPE_THETA ** (-jnp.arange(0, A_HEAD_DIM, 2, dtype=F32) / A_HEAD_DIM)
    ang = pos.astype(F32)[:, None] * inv_freq[None, :]
    cos = jnp.concatenate([jnp.cos(ang)] * (2 * LANES // A_HEAD_DIM), axis=-1)
    sin = jnp.concatenate([jnp.sin(ang)] * (2 * LANES // A_HEAD_DIM), axis=-1)
    return cos, sin


def _dispatch_tables(idx, rank, counts, sub):
    k, m = idx.shape
    rows_g = MOE_GROUP_ROWS
    padded = (counts + rows_g - 1) // rows_g * rows_g
    pad_end = jnp.cumsum(padded)
    pad_start = pad_end - padded
    n_groups = -(-(k * m) // rows_g) + N_EXPERTS
    onehot = idx[:, :, None] == jnp.arange(N_EXPERTS, dtype=I32)[None, None, :]
    pos = jnp.sum(jnp.where(onehot, pad_start[None, None, :], 0), axis=-1) + rank
    group_start = jnp.arange(n_groups, dtype=I32) * rows_g
    group_expert = jnp.minimum(jnp.sum(pad_end[None, :] <= group_start[:, None], axis=1), N_EXPERTS - 1).astype(I32)
    n_used = (pad_end[-1:] // rows_g).astype(I32)
    pos_tiles = (pos * sub).reshape(k, m // ROW_TILE, ROW_TILE).transpose(1, 0, 2).reshape(-1).astype(I32)
    return group_expert, n_used, pad_start.astype(I32), pos_tiles, n_groups


def kernel(x_prompt, x_sample, cache_k, cache_v, state_gla, page_table, c_prompt, c_sample, w_ada, b_ada,
           norm_mix_w, norm_ffn_w, norm_final_w, w_in, w_gate_up, b_gate, gla_norm_w, w_proj_a, w_proj_b, w_out,
           w_router, b_router, w_e1, w_e3, w_e2, w_s1, w_s3, w_s2):
    batch, seq, d = x_prompt.shape
    n_dec, t_new, _ = x_sample.shape
    depth = w_ada.shape[0]
    n_pool, page = cache_k.shape[1], cache_k.shape[2]
    pages_per_seq = page_table.shape[1]
    past_len = pages_per_seq * page
    assert depth == 1 and past_len % MOBA_BLOCK == 0 and MOBA_BLOCK % page == 0
    assert seq % ROW_TILE == 0 and ROW_TILE % MOBA_BLOCK == 0 and ROW_TILE % t_new == 0 and t_new == SUBLANES
    assert seq % ATTN_KEY_CHUNK == 0 and ATTN_KEY_CHUNK % MOBA_BLOCK == 0 and (m_all := batch * seq + n_dec * t_new) % ROW_TILE == 0
    del m_all
    m_p, m_s = batch * seq, n_dec * t_new
    bkw = w_gate_up.shape[2]
    dk = bkw // B_HEADS
    dv = gla_norm_w.shape[1]
    bvw = B_HEADS * dv

    n_c = batch + n_dec
    n_c_pad = -(-n_c // SUBLANES) * SUBLANES
    c_all = jnp.concatenate([c_prompt, c_sample, jnp.zeros((n_c_pad - n_c, d), F32)], axis=0)
    mod = _ada_call(c_all, w_ada[0], b_ada[0])
    mod_p = mod[:batch].reshape(batch, 1, 6 * d)
    mod_s = mod[batch:n_c].reshape(n_dec, 1, 6 * d)

    sizes = (A_WIDTH, A_WIDTH, A_WIDTH, bkw, bkw, bvw, bvw, GATE_RANK, d, d)
    offs = [0]
    for s in sizes:
        offs.append(offs[-1] + s)
    w_in0 = w_in[0]
    w_main = jnp.concatenate([w_in0[:, :offs[7]], w_in0[:, offs[8]:]], axis=1).astype(BF16)
    w_gr = jnp.pad(w_in0[:, offs[7]:offs[8]], ((0, 0), (0, LANES - GATE_RANK))).astype(BF16)
    wg = jnp.pad(w_gate_up[0], ((0, LANES - GATE_RANK), (0, 0)))
    bg = b_gate[0].reshape(1, bkw)
    nw1 = norm_mix_w[0].reshape(1, d)
    nw2 = norm_ffn_w[0].reshape(1, d)
    nwf = norm_final_w.reshape(1, d)
    gnw = gla_norm_w[0].reshape(1, dv)
    wpa = w_proj_a[0].astype(BF16).reshape(A_HEADS, A_HEAD_DIM, d)
    wpb = w_proj_b[0].astype(BF16)
    wout = w_out[0].astype(BF16)
    wr_t = w_router[0].T
    br_lanes = jnp.broadcast_to(b_router[0][:, None], (N_EXPERTS, LANES))
    we1, we3, we2 = w_e1[0].astype(BF16), w_e3[0].astype(BF16), w_e2[0].astype(BF16)
    ws1, ws3, ws2 = w_s1[0].astype(BF16), w_s3[0].astype(BF16), w_s2[0].astype(BF16)

    tiles_per_seq = seq // ROW_TILE
    groups_s = ROW_TILE // t_new

    cos_p, sin_p = _rope_tables(jnp.arange(seq, dtype=I32))
    xp3 = x_prompt.reshape(m_p // ROW_TILE, ROW_TILE, d)
    (qa_p, ka_p, va_p, qb_p, kb_p, vb_p, gb_p, ga_p, gm_p, gr_p, kmean_p) = _inproj_call(
        xp3, mod_p, nw1, cos_p, sin_p, w_main, w_gr, groups_per_tile=1, rows_per_group=ROW_TILE,
        mod_of_tile=lambda i: i // tiles_per_seq, cos_of_tile=lambda i: i % tiles_per_seq, with_kmean=True,
        name="inproj_prompt")
    qaug, kaug, vaug = _select_call(qa_p, ka_p, va_p, kmean_p, batch, seq)
    oa_p = _moba_attn_call(qaug, kaug, vaug)
    ob_p, state_p = _gla_call(qb_p, kb_p, vb_p, gr_p, wg, bg, None, n_seq=batch, seq_len=seq, chunk=GLA_CHUNK,
                              chunks_per_step=GLA_CHUNKS_PER_STEP, name="gla_prompt")
    x1_p, h2_p, lg_p = _mixer_call(
        xp3, oa_p, ob_p, gb_p, ga_p, gm_p, mod_p, nw2, gnw, wpa, wpb, wout, wr_t, groups_per_tile=1,
        mod_of_tile=lambda i: i // tiles_per_seq,
        oa_of_tile=lambda i: (i // tiles_per_seq, 0, i % tiles_per_seq, 0), name="mixer_prompt")

    cos_s, sin_s = _rope_tables(past_len + jnp.arange(t_new, dtype=I32))
    cos_s = jnp.tile(cos_s, (groups_s, 1))
    sin_s = jnp.tile(sin_s, (groups_s, 1))
    (qa_s, ka_s, va_s, qb_s, kb_s, vb_s, gb_s, ga_s, gm_s, gr_s) = _inproj_call(
        x_sample, mod_s, nw1, cos_s, sin_s, w_main, w_gr, groups_per_tile=groups_s, rows_per_group=t_new,
        mod_of_tile=lambda i: i, cos_of_tile=lambda i: 0, with_kmean=False, name="inproj_sample")
    pt_flat = page_table.reshape(-1)
    ck4 = cache_k[0].transpose(0, 2, 3, 1)
    cv4 = cache_v[0].transpose(0, 2, 3, 1)
    o_c = _paged_attn_call(pt_flat, ck4, cv4, qa_s, ka_s, va_s, n_dec, pages_per_seq, t_new)
    o5 = o_c.reshape(n_dec, A_HEADS, t_new, A_HEADS, A_HEAD_DIM)
    hh = jnp.arange(A_HEADS)
    oa_s = o5[:, hh, :, hh, :].reshape(1, A_HEADS, m_s, A_HEAD_DIM)
    ob_s, state_s = _gla_call(qb_s, kb_s, vb_s, gr_s, wg, bg, state_gla[0], n_seq=n_dec, seq_len=t_new, chunk=t_new,
                              chunks_per_step=1, name="gla_sample")
    x1_s, h2_s, lg_s = _mixer_call(
        x_sample, oa_s, ob_s, gb_s, ga_s, gm_s, mod_s, nw2, gnw, wpa, wpb, wout, wr_t, groups_per_tile=groups_s,
        mod_of_tile=lambda i: i, oa_of_tile=lambda i: (0, 0, i, 0), name="mixer_sample")

    sub = d // LANES
    h2g = jnp.concatenate([h2_p, h2_s], axis=0)
    idx, gw, rank, cnt = _route_call(jnp.concatenate([lg_p, lg_s], axis=1), br_lanes)
    counts = cnt[:, 0]
    group_expert, n_used, pad_start, pos_tiles, n_groups = _dispatch_tables(idx, rank, counts, sub)
    xsorted = _dispatch_call(counts, pad_start, n_used, pos_tiles, h2g, n_groups * MOE_GROUP_ROWS, d)
    ysorted = _expert_call(group_expert, n_used, xsorted, we1, we3, we2)
    n_tiles_p = m_p // ROW_TILE
    y_p = _final_call(pos_tiles, gw, x1_p, h2g, mod_p, nwf, ws1, ws3, ws2, ysorted, groups_per_tile=1,
                      mod_of_tile=lambda i: i // tiles_per_seq, first_tile=0, name="final_prompt")
    y_s = _final_call(pos_tiles, gw, x1_s, h2g, mod_s, nwf, ws1, ws3, ws2, ysorted, groups_per_tile=groups_s,
                      mod_of_tile=lambda i: i, first_tile=n_tiles_p, name="final_sample")

    return (
        y_p.reshape(batch, seq, d),
        y_s,
        ka_p.reshape(1, batch, seq, A_HEADS, A_HEAD_DIM),
        va_p.reshape(1, batch, seq, A_HEADS, A_HEAD_DIM),
        state_p[None],
        ka_s.reshape(1, n_dec, t_new, A_HEADS, A_HEAD_DIM),
        va_s.reshape(1, n_dec, t_new, A_HEADS, A_HEAD_DIM),
        state_s[None],
    )
```

```python
import functools

import jax
import jax.numpy as jnp
from jax import lax
from jax.experimental import pallas as pl
from jax.experimental.pallas import tpu as pltpu

F32 = jnp.float32
BF16 = jnp.bfloat16
I32 = jnp.int32
HIGHEST = lax.Precision.HIGHEST

A_HEADS = 8
A_HEAD_DIM = 64
A_WIDTH = A_HEADS * A_HEAD_DIM
MOBA_BLOCK = 256
MOBA_TOPK = 3
ROPE_THETA = 10000.0
B_HEADS = 4
GATE_RANK = 16
GATE_NORMALIZER = 16.0
N_EXPERTS = 64
N_GROUPS = 8
GROUP_SIZE = N_EXPERTS // N_GROUPS
TOPK_GROUPS = 4
TOP_K = 8
ROUTED_SCALE = 2.5
EPS = 1e-6
NEG = -1e30
BELOW_NEG = -3e38

LANES = 128
SUBLANES = 8
VMEM_BYTES = 64 * 1024 * 1024

LOG2_E = 1.4426950408889634
ATTN_KEY_CHUNK = 2048
ATTN_HEADS_PER_STEP = 2

PAGED_ATTN_PAGES_PER_STEP = 32

ROW_TILE = 256
GLA_CHUNK = 64
GLA_CHUNKS_PER_STEP = 4
MOE_GROUP_ROWS = 256


def _cparams(semantics, vmem_mb):
    return pltpu.CompilerParams(dimension_semantics=semantics, vmem_limit_bytes=vmem_mb * 1024 * 1024)


def _silu(x):
    return x * jax.nn.sigmoid(x)


def _dot(a, b):
    return jnp.dot(a, b, preferred_element_type=F32)


def _dot_nt(a, b, precision=None):
    return lax.dot_general(a, b, (((1,), (1,)), ((), ())), precision=precision, preferred_element_type=F32)


def _dot_tn(a, b):
    return lax.dot_general(a, b, (((0,), (0,)), ((), ())), preferred_element_type=F32)


def _first_argmax(vals, iota, axis, size):
    mx = jnp.max(vals, axis=axis, keepdims=True)
    return jnp.min(jnp.where(vals == mx, iota, size), axis=axis, keepdims=True)


def _store_row_tiles(ref, x):
    rows, d = x.shape
    sub = d // LANES
    for s in range(sub):
        ref[pl.ds(s, rows, stride=sub), :] = x[:, s * LANES:(s + 1) * LANES]


def _load_row_tiles(ref, rows, sub):
    return jnp.concatenate([ref[pl.ds(s, rows, stride=sub), :] for s in range(sub)], axis=1)


def _pack_bf16_pairs(x):
    half = x.shape[1] // 2
    hi = lax.bitcast_convert_type(x[:, :half].astype(BF16).astype(F32), jnp.uint32)
    lo = lax.bitcast_convert_type(x[:, half:].astype(BF16).astype(F32), jnp.uint32)
    return hi | (lo >> 16)


def _unpack_bf16_pairs(w):
    hi = lax.bitcast_convert_type(w & jnp.uint32(0xFFFF0000), F32)
    lo = lax.bitcast_convert_type(w << 16, F32)
    return jnp.concatenate([hi, lo], axis=1).astype(BF16)


def _ada_kernel(c_ref, w_ref, b_ref, o_ref):
    s = _silu(c_ref[...])
    o_ref[...] = jnp.dot(s, w_ref[...], precision=HIGHEST, preferred_element_type=F32) + b_ref[...]


def _ada_call(c_all, w_ada, b_ada):
    n, d = c_all.shape
    width = w_ada.shape[1]
    tn = width // 4
    return pl.pallas_call(
        _ada_kernel,
        grid=(width // tn,),
        in_specs=[
            pl.BlockSpec((n, d), lambda j: (0, 0)),
            pl.BlockSpec((d, tn), lambda j: (0, j)),
            pl.BlockSpec((1, tn), lambda j: (0, j)),
        ],
        out_specs=pl.BlockSpec((n, tn), lambda j: (0, j)),
        out_shape=jax.ShapeDtypeStruct((n, width), F32),
        compiler_params=_cparams(("arbitrary",), 40),
        name="adaln",
    )(c_all, w_ada, b_ada.reshape(1, width))


def _rope(a, cos, sin):
    lane = lax.broadcasted_iota(I32, cos.shape, 1)
    first_half = (lane % A_HEAD_DIM) < (A_HEAD_DIM // 2)
    outs = []
    for c in range(a.shape[1] // LANES):
        xc = a[:, c * LANES:(c + 1) * LANES]
        rot = jnp.where(first_half, -pltpu.roll(xc, LANES - A_HEAD_DIM // 2, 1), pltpu.roll(xc, A_HEAD_DIM // 2, 1))
        outs.append(xc * cos + rot * sin)
    return jnp.concatenate(outs, axis=1)


def _modulated_norm(x, norm_w, scale, shift):
    ms = jnp.mean(x * x, axis=-1, keepdims=True)
    y = x * lax.rsqrt(ms + EPS) * norm_w
    return y * (1.0 + scale) + shift


def _inproj_kernel(x_ref, sh_ref, sc_ref, nw_ref, cos_ref, sin_ref, w_ref, wgr_ref,
                   q_ref, k_ref, v_ref, qb_ref, kb_ref, vb_ref, gb_ref, ga_ref, gm_ref, gr_ref, *km_refs):
    g, r, d = x_ref.shape
    h = _modulated_norm(x_ref[...], nw_ref[...], sc_ref[...], sh_ref[...])
    hb = h.reshape(g * r, d).astype(BF16)
    cos = cos_ref[...]
    sin = sin_ref[...]
    col = 0

    def proj(width):
        nonlocal col
        out = _dot(hb, w_ref[:, col:col + width])
        col += width
        return out

    q_ref[...] = _rope(proj(A_WIDTH), cos, sin)
    k = _rope(proj(A_WIDTH), cos, sin)
    k_ref[...] = k
    if km_refs:
        nb = (g * r) // MOBA_BLOCK
        km_refs[0][...] = jnp.mean(k.reshape(nb, MOBA_BLOCK, A_WIDTH), axis=1, keepdims=True)
    v_ref[...] = proj(A_WIDTH)
    for ref in (qb_ref, kb_ref, vb_ref, gb_ref, ga_ref, gm_ref):
        ref[...] = proj(ref.shape[1])
    gr_ref[...] = _dot(hb, wgr_ref[...])


def _inproj_call(x3, mod3, norm_w, cos, sin, w_main, w_gr, *, groups_per_tile, rows_per_group, mod_of_tile,
                 cos_of_tile, with_kmean, name):
    n_groups, r, d = x3.shape
    assert r == rows_per_group
    m = n_groups * r
    tm = groups_per_tile * r
    n_tiles = n_groups // groups_per_tile
    d_model = d
    bk = w_main.shape[1] - 3 * A_WIDTH - 4 * d_model
    widths = [A_WIDTH, A_WIDTH, A_WIDTH, bk // 2, bk // 2, d_model, d_model, d_model, d_model, LANES]
    out_shape = [jax.ShapeDtypeStruct((m, w), F32) for w in widths]
    out_specs = [pl.BlockSpec((tm, w), lambda i: (i, 0)) for w in widths]
    if with_kmean:
        out_shape.append(jax.ShapeDtypeStruct((m // MOBA_BLOCK, 1, A_WIDTH), F32))
        out_specs.append(pl.BlockSpec((tm // MOBA_BLOCK, 1, A_WIDTH), lambda i: (i, 0, 0)))
    resident = dict(pipeline_mode=pl.Buffered(1))
    return pl.pallas_call(
        _inproj_kernel,
        grid=(n_tiles,),
        in_specs=[
            pl.BlockSpec((groups_per_tile, r, d), lambda i: (i, 0, 0)),
            pl.BlockSpec((groups_per_tile, 1, d), lambda i: (mod_of_tile(i), 0, 0)),
            pl.BlockSpec((groups_per_tile, 1, d), lambda i: (mod_of_tile(i), 0, 1)),
            pl.BlockSpec((1, d), lambda i: (0, 0), **resident),
            pl.BlockSpec((tm, LANES), lambda i: (cos_of_tile(i), 0)),
            pl.BlockSpec((tm, LANES), lambda i: (cos_of_tile(i), 0)),
            pl.BlockSpec(w_main.shape, lambda i: (0, 0), **resident),
            pl.BlockSpec(w_gr.shape, lambda i: (0, 0), **resident),
        ],
        out_specs=out_specs,
        out_shape=out_shape,
        compiler_params=_cparams(("parallel",), 48),
        name=name,
    )(x3, mod3, mod3, norm_w, cos, sin, w_main, w_gr)


def _select_kernel(q_ref, k_ref, v_ref, km_ref, qa_ref, ka_ref, va_ref, *, nbw):
    i = pl.program_id(1)
    q = q_ref[...]
    k = k_ref[...]
    v = v_ref[...]
    km = km_ref[:, 0, :]
    nblk = km.shape[0]
    rows = q.shape[0]
    q_t = q.T
    v_t = v.T
    blk = lax.broadcasted_iota(I32, (nblk, rows), 0)
    colw = lax.broadcasted_iota(I32, (rows, nbw), 1)
    ones_row = (lax.broadcasted_iota(I32, (A_HEAD_DIM, rows), 0) == 0).astype(F32)
    own_onehot = jnp.where(colw == i, -NEG, 0.0)
    for h in range(A_HEADS):
        sl = slice(h * A_HEAD_DIM, (h + 1) * A_HEAD_DIM)
        s = _dot_nt(km[:, sl], q[:, sl], precision=HIGHEST)
        s = jnp.where(blk < i, s, NEG)
        sel = jnp.zeros(s.shape, jnp.bool_)
        for _ in range(MOBA_TOPK):
            first = _first_argmax(s, blk, 0, nblk)
            hit = blk == first
            sel = sel | (hit & (first < i))
            s = jnp.where(hit, BELOW_NEG, s)
        selneg = jnp.where(sel, 0.0, -1.0)
        if nbw > nblk:
            selneg = jnp.concatenate([selneg, jnp.full((nbw - nblk, rows), -1.0, F32)], axis=0)
        qa_ref[0, h] = jnp.concatenate([q_t[sl, :] * (A_HEAD_DIM ** -0.5 * LOG2_E), selneg], axis=0).astype(BF16)
        ka_ref[0, h] = jnp.concatenate([k[:, sl], own_onehot], axis=1).astype(BF16)
        va_ref[0, h] = jnp.concatenate([v_t[sl, :], ones_row], axis=0).astype(BF16)


def _select_call(q, k, v, kmean, batch, seq):
    nblk = seq // MOBA_BLOCK
    nbw = -(-nblk // A_HEAD_DIM) * A_HEAD_DIM
    aw = A_HEAD_DIM + nbw
    row_spec = pl.BlockSpec((MOBA_BLOCK, A_WIDTH), lambda b, i: (b * nblk + i, 0))
    t_spec = lambda w: pl.BlockSpec((1, A_HEADS, w, MOBA_BLOCK), lambda b, i: (b, 0, 0, i))
    return pl.pallas_call(
        functools.partial(_select_kernel, nbw=nbw),
        grid=(batch, nblk),
        in_specs=[row_spec, row_spec, row_spec, pl.BlockSpec((nblk, 1, A_WIDTH), lambda b, i: (b, 0, 0))],
        out_specs=[t_spec(aw), pl.BlockSpec((1, A_HEADS, MOBA_BLOCK, aw), lambda b, i: (b, 0, i, 0)),
                   t_spec(2 * A_HEAD_DIM)],
        out_shape=[
            jax.ShapeDtypeStruct((batch, A_HEADS, aw, seq), BF16),
            jax.ShapeDtypeStruct((batch, A_HEADS, seq, aw), BF16),
            jax.ShapeDtypeStruct((batch, A_HEADS, 2 * A_HEAD_DIM, seq), BF16),
        ],
        compiler_params=_cparams(("parallel", "parallel"), 32),
        name="moba_select",
    )(q, k, v, kmean)


def _moba_attn_kernel(qa_ref, ka_ref, va_ref, kown_ref, vown_ref, o_ref, sa_ref, sb_ref, mxa_ref, mxb_ref,
                      m_ref, acc_ref):
    i = pl.program_id(2)
    chunk = ATTN_KEY_CHUNK
    heads = range(qa_ref.shape[1])

    def col_max(s_t):
        while s_t.shape[0] > SUBLANES:
            half = s_t.shape[0] // 2
            s_t = jnp.maximum(s_t[:half], s_t[half:])
        return s_t

    n_trips = (i * MOBA_BLOCK + chunk - 1) // chunk
    last_chunk = ka_ref.shape[2] // chunk - 1

    def stage_logits(buf, h, c):
        r0 = pl.multiple_of(jnp.minimum(c, last_chunk) * chunk, chunk)
        s_t = _dot(ka_ref[0, h, pl.ds(r0, chunk), :], qa_ref[0, h])
        buf[0][h] = s_t
        buf[1][h] = col_max(s_t)

    def consume(buf, h, c):
        r0 = pl.multiple_of(c * chunk, chunk)
        m_prev = m_ref[h, 0:1, :]
        m_new = jnp.maximum(m_prev, jnp.max(buf[1][h], axis=0, keepdims=True))
        alpha = jnp.exp2(m_prev - m_new)
        p_t = jnp.exp2(buf[0][h] - m_new).astype(BF16)
        acc_ref[h] = alpha * acc_ref[h] + _dot(va_ref[0, h, :, pl.ds(r0, chunk)], p_t)
        m_ref[h] = jnp.broadcast_to(m_new, m_ref.shape[1:])

    buf_a, buf_b = (sa_ref, mxa_ref), (sb_ref, mxb_ref)
    for h in heads:
        stage_logits(buf_a, h, 0)
    for h in heads:
        s_t = _dot(kown_ref[0, h][:, 0:A_HEAD_DIM], qa_ref[0, h][0:A_HEAD_DIM, :])
        key = lax.broadcasted_iota(I32, s_t.shape, 0)
        qry = lax.broadcasted_iota(I32, s_t.shape, 1)
        s_t = jnp.where(key <= qry, s_t, NEG)
        m_own = jnp.max(col_max(s_t), axis=0, keepdims=True)
        m_ref[h] = jnp.broadcast_to(m_own, m_ref.shape[1:])
        acc_ref[h] = _dot(vown_ref[0, h], jnp.exp2(s_t - m_own).astype(BF16))

    def past_chunk_pair(t, carry):
        c = 2 * t
        for h in heads:
            stage_logits(buf_b, h, c + 1)
        for h in heads:
            consume(buf_a, h, c)

        @pl.when(c + 1 < n_trips)
        def _():
            for h in heads:
                stage_logits(buf_a, h, c + 2)
            for h in heads:
                consume(buf_b, h, c + 1)

        return carry

    lax.fori_loop(0, (n_trips + 1) // 2, past_chunk_pair, 0)
    outs = []
    for h in heads:
        acc = acc_ref[h].T
        outs.append(acc[:, :A_HEAD_DIM] / acc[:, A_HEAD_DIM:A_HEAD_DIM + 1])
    o_ref[0] = jnp.concatenate(outs, axis=1)


def _moba_attn_call(qaug_t, kaug, vaug_t):
    batch, heads, seq, aw = kaug.shape
    nblk = seq // MOBA_BLOCK
    vw = vaug_t.shape[2]
    hs = ATTN_HEADS_PER_STEP
    return pl.pallas_call(
        _moba_attn_kernel,
        grid=(batch, heads // hs, nblk),
        in_specs=[
            pl.BlockSpec((1, hs, aw, MOBA_BLOCK), lambda b, h, i: (b, h, 0, i)),
            pl.BlockSpec((1, hs, seq, aw), lambda b, h, i: (b, h, 0, 0)),
            pl.BlockSpec((1, hs, vw, seq), lambda b, h, i: (b, h, 0, 0)),
            pl.BlockSpec((1, hs, MOBA_BLOCK, aw), lambda b, h, i: (b, h, i, 0)),
            pl.BlockSpec((1, hs, vw, MOBA_BLOCK), lambda b, h, i: (b, h, 0, i)),
        ],
        out_specs=pl.BlockSpec((1, MOBA_BLOCK, hs * A_HEAD_DIM), lambda b, h, i: (b, i, h)),
        out_shape=jax.ShapeDtypeStruct((batch, seq, heads * A_HEAD_DIM), F32),
        scratch_shapes=[pltpu.VMEM((hs, ATTN_KEY_CHUNK, MOBA_BLOCK), F32),
                        pltpu.VMEM((hs, ATTN_KEY_CHUNK, MOBA_BLOCK), F32),
                        pltpu.VMEM((hs, SUBLANES, MOBA_BLOCK), F32),
                        pltpu.VMEM((hs, SUBLANES, MOBA_BLOCK), F32),
                        pltpu.VMEM((hs, SUBLANES, MOBA_BLOCK), F32), pltpu.VMEM((hs, vw, MOBA_BLOCK), F32)],
        compiler_params=_cparams(("parallel", "parallel", "arbitrary"), 56),
        name="moba_attention",
    )(qaug_t, kaug, vaug_t, kaug, vaug_t)


def _gla_level_masks(c, dk):
    row = lax.broadcasted_iota(I32, (c, c), 0)
    col = lax.broadcasted_iota(I32, (c, c), 1)
    r1 = lax.broadcasted_iota(I32, (c, dk), 0)
    levels = []
    half = c // 2
    while half >= 1:
        grp = 2 * half
        valid = (row // grp == col // grp) & (row % grp >= half) & (col % grp < half)
        picks = {} if grp >= SUBLANES else {
            off: r1 % grp == half - 1 + off for off in range(-(half - 1), half + 1) if off != 0}
        levels.append((half, valid, picks))
        half //= 2
    return row == col, levels


def _gla_intra_scores(q, k, b, masks):
    c, dk = q.shape
    diagonal, levels = masks
    scores = jnp.where(diagonal, _dot_nt(q.astype(BF16), k.astype(BF16)), 0.0)
    for half, valid, picks in levels:
        grp = 2 * half
        if grp >= SUBLANES:
            a = jnp.concatenate([jnp.broadcast_to(b[j * grp + half - 1:j * grp + half, :], (grp, dk))
                                 for j in range(c // grp)], axis=0)
        else:
            a = b
            for off, pick in picks.items():
                a = jnp.where(pick, pltpu.roll(b, off % c, 0), a)
        qf = (q * jnp.exp(jnp.minimum(b - a, 0.0))).astype(BF16)
        kf = (k * jnp.exp(jnp.minimum(a - b, 0.0))).astype(BF16)
        scores = jnp.where(valid, _dot_nt(qf, kf), scores)
    return scores


def _gla_kernel(*refs, chunk, n_chunks, has_s0):
    if has_s0:
        q_ref, k_ref, v_ref, gr_ref, wg_ref, bg_ref, s0_ref, o_ref, sout_ref, st_ref = refs
    else:
        q_ref, k_ref, v_ref, gr_ref, wg_ref, bg_ref, o_ref, sout_ref, st_ref = refs
    step = pl.program_id(1)
    heads, dv, dk = st_ref.shape

    @pl.when(step == 0)
    def _():
        for h in range(heads):
            if has_s0:
                st_ref[h] = s0_ref[0, h].T
            else:
                st_ref[h] = jnp.zeros((dv, dk), F32)

    tri = (lax.broadcasted_iota(I32, (chunk, chunk), 1) <= lax.broadcasted_iota(I32, (chunk, chunk), 0)).astype(F32)
    masks = _gla_level_masks(chunk, dk)
    for c in range(n_chunks):
        sl = slice(c * chunk, (c + 1) * chunk)
        z = jnp.dot(gr_ref[sl, :], wg_ref[...], precision=HIGHEST, preferred_element_type=F32) + bg_ref[...]
        log_a = -(jnp.maximum(-z, 0.0) + jnp.log1p(jnp.exp(-jnp.abs(z)))) * (1.0 / GATE_NORMALIZER)
        b_all = jnp.dot(tri, log_a, precision=HIGHEST, preferred_element_type=F32)
        for h in range(heads):
            kl = slice(h * dk, (h + 1) * dk)
            vl = slice(h * dv, (h + 1) * dv)
            q = q_ref[sl, kl] * (dk ** -0.5)
            k = k_ref[sl, kl]
            v = v_ref[sl, vl].astype(BF16)
            b = b_all[:, kl]
            b_last = b[chunk - 1:chunk, :]
            st = st_ref[h]
            scores = _gla_intra_scores(q, k, b, masks)
            inter = _dot_nt((q * jnp.exp(b)).astype(BF16), st.astype(BF16))
            o_ref[sl, vl] = _dot(scores.astype(BF16), v) + inter
            kd = (k * jnp.exp(b_last - b)).astype(BF16)
            st_ref[h] = st * jnp.exp(b_last) + _dot_tn(v, kd)

    @pl.when(step == pl.num_programs(1) - 1)
    def _():
        for h in range(heads):
            sout_ref[0, h] = st_ref[h].T


def _gla_call(qb, kb, vb, gr, wg, bg, s0, *, n_seq, seq_len, chunk, chunks_per_step, name):
    heads = B_HEADS
    kw, vw = qb.shape[1], vb.shape[1]
    dk, dv = kw // heads, vw // heads
    rb = chunk * chunks_per_step
    steps = seq_len // rb
    rows = lambda w: pl.BlockSpec((rb, w), lambda n, s: (n * steps + s, 0))
    state = pl.BlockSpec((1, heads, dk, dv), lambda n, s: (n, 0, 0, 0))
    in_specs = [rows(kw), rows(kw), rows(vw), rows(LANES),
                pl.BlockSpec(wg.shape, lambda n, s: (0, 0)), pl.BlockSpec(bg.shape, lambda n, s: (0, 0))]
    args = [qb, kb, vb, gr, wg, bg]
    if s0 is not None:
        in_specs.append(state)
        args.append(s0)
    return pl.pallas_call(
        functools.partial(_gla_kernel, chunk=chunk, n_chunks=chunks_per_step, has_s0=s0 is not None),
        grid=(n_seq, steps),
        in_specs=in_specs,
        out_specs=[rows(vw), state],
        out_shape=[
            jax.ShapeDtypeStruct((n_seq * seq_len, vw), F32),
            jax.ShapeDtypeStruct((n_seq, heads, dk, dv), F32),
        ],
        scratch_shapes=[pltpu.VMEM((heads, dv, dk), F32)],
        compiler_params=_cparams(("parallel", "arbitrary"), 40),
        name=name,
    )(*args)


def _mixer_kernel(x_ref, oa_ref, ob_ref, gb_ref, ga_ref, gm_ref, g1_ref, sh2_ref, sc2_ref, nw2_ref, gnw_ref,
                  wpa_ref, wpb_ref, wout_ref, wr_ref, x1_ref, h2_ref, lg_ref):
    g, r, d = x_ref.shape
    tm = g * r
    dv = gnw_ref.shape[1]
    ob = ob_ref[...]
    gb = gb_ref[...]
    gnw = gnw_ref[...]
    parts = []
    for h in range(ob.shape[1] // dv):
        oh = ob[:, h * dv:(h + 1) * dv]
        ms = jnp.mean(oh * oh, axis=-1, keepdims=True)
        parts.append((oh * lax.rsqrt(ms + EPS) * gnw * _silu(gb[:, h * dv:(h + 1) * dv])).astype(BF16))
    pb = _dot(jnp.concatenate(parts, axis=1), wpb_ref[...])
    pa = _dot(oa_ref[...].astype(BF16), wpa_ref[...])
    merged = jax.nn.sigmoid(ga_ref[...]) * pa + jax.nn.sigmoid(gm_ref[...]) * pb
    u = _dot(merged.astype(BF16), wout_ref[...])
    x1 = x_ref[...] + g1_ref[...] * u.reshape(g, r, d)
    x1_ref[...] = x1
    h2 = _modulated_norm(x1, nw2_ref[...], sc2_ref[...], sh2_ref[...]).reshape(tm, d)
    _store_row_tiles(h2_ref, _pack_bf16_pairs(h2))
    lg_ref[...] = _dot_nt(wr_ref[...], h2, precision=HIGHEST)


def _mixer_call(x3, oa, ob, gb, ga, gm, mod3, nw2, gnw, wpa, wpb, wout, wr_t, *, groups_per_tile, mod_of_tile, name):
    n_groups, r, d = x3.shape
    m = n_groups * r
    tm = groups_per_tile * r
    n_tiles = n_groups // groups_per_tile
    row = lambda w: pl.BlockSpec((tm, w), lambda i: (i, 0))
    modspec = lambda k: pl.BlockSpec((groups_per_tile, 1, d), lambda i: (mod_of_tile(i), 0, k))
    const = lambda a: pl.BlockSpec(a.shape, lambda i: (0,) * a.ndim, pipeline_mode=pl.Buffered(1))
    return pl.pallas_call(
        _mixer_kernel,
        grid=(n_tiles,),
        in_specs=[
            pl.BlockSpec((groups_per_tile, r, d), lambda i: (i, 0, 0)),
            row(oa.shape[1]), row(ob.shape[1]), row(gb.shape[1]), row(d), row(d),
            modspec(2), modspec(3), modspec(4),
            const(nw2), const(gnw), const(wpa), const(wpb), const(wout), const(wr_t),
        ],
        out_specs=[
            pl.BlockSpec((groups_per_tile, r, d), lambda i: (i, 0, 0)),
            pl.BlockSpec((tm * d // (2 * LANES), LANES), lambda i: (i, 0)),
            pl.BlockSpec((N_EXPERTS, tm), lambda i: (0, i)),
        ],
        out_shape=[
            jax.ShapeDtypeStruct((n_groups, r, d), F32),
            jax.ShapeDtypeStruct((m * d // (2 * LANES), LANES), jnp.uint32),
            jax.ShapeDtypeStruct((N_EXPERTS, m), F32),
        ],
        compiler_params=_cparams(("parallel",), 48),
        name=name,
    )(x3, oa, ob, gb, ga, gm, mod3, mod3, mod3, nw2, gnw, wpa, wpb, wout, wr_t)


def _route_kernel(lg_ref, br_ref, idx_ref, gw_ref, rank_ref, cnt_ref, carry_ref):
    @pl.when(pl.program_id(0) == 0)
    def _():
        carry_ref[...] = jnp.zeros(carry_ref.shape, F32)

    logits = lg_ref[...]
    n_e, tm = logits.shape
    scores = jax.nn.sigmoid(logits)
    biased = scores + jnp.concatenate([br_ref[...]] * (tm // LANES), axis=1)
    g3 = biased.reshape(N_GROUPS, GROUP_SIZE, tm)
    sub = lax.broadcasted_iota(I32, g3.shape, 1)
    m1 = jnp.max(g3, axis=1, keepdims=True)
    f1 = jnp.min(jnp.where(g3 == m1, sub, GROUP_SIZE), axis=1, keepdims=True)
    m2 = jnp.max(jnp.where(sub == f1, -jnp.inf, g3), axis=1, keepdims=True)
    gs = (m1 + m2).reshape(N_GROUPS, tm)
    giota = lax.broadcasted_iota(I32, gs.shape, 0)
    gmask = jnp.zeros(gs.shape, jnp.bool_)
    for _ in range(TOPK_GROUPS):
        first = _first_argmax(gs, giota, 0, N_GROUPS)
        hit = giota == first
        gmask = gmask | hit
        gs = jnp.where(hit, -jnp.inf, gs)
    emask = jnp.broadcast_to(gmask.reshape(N_GROUPS, 1, tm), (N_GROUPS, GROUP_SIZE, tm)).reshape(n_e, tm)
    masked = jnp.where(emask, biased, -jnp.inf)
    eiota = lax.broadcasted_iota(I32, masked.shape, 0)
    idx_rows, w_rows = [], []
    for _ in range(TOP_K):
        first = _first_argmax(masked, eiota, 0, n_e)
        hit = eiota == first
        idx_rows.append(first)
        w_rows.append(jnp.sum(jnp.where(hit, scores, 0.0), axis=0, keepdims=True))
        masked = jnp.where(hit, -jnp.inf, masked)
    w = jnp.concatenate(w_rows, axis=0)
    idx_ref[...] = jnp.concatenate(idx_rows, axis=0)
    gw_ref[...] = w / jnp.sum(w, axis=0, keepdims=True) * ROUTED_SCALE
    chosen = jnp.zeros(masked.shape, F32)
    for first in idx_rows:
        chosen = chosen + (eiota == first).astype(F32)
    earlier = (lax.broadcasted_iota(I32, (tm, tm), 0) < lax.broadcasted_iota(I32, (tm, tm), 1)).astype(BF16)
    carry = carry_ref[...]
    base = _dot(chosen.astype(BF16), earlier) + jnp.concatenate([carry] * (tm // LANES), axis=1)
    rank_ref[...] = jnp.concatenate(
        [jnp.sum(jnp.where(eiota == first, base, 0.0), axis=0, keepdims=True) for first in idx_rows], axis=0).astype(I32)
    carry = carry + jnp.sum(chosen, axis=1, keepdims=True)
    carry_ref[...] = carry
    cnt_ref[...] = carry.astype(I32)


def _route_call(logits_t, b_router_lanes):
    n_e, m = logits_t.shape
    tm = 512 if m % 512 == 0 else ROW_TILE
    tok = lambda dt: jax.ShapeDtypeStruct((TOP_K, m), dt)
    tok_spec = pl.BlockSpec((TOP_K, tm), lambda i: (0, i))
    return pl.pallas_call(
        _route_kernel,
        grid=(m // tm,),
        in_specs=[pl.BlockSpec((n_e, tm), lambda i: (0, i)), pl.BlockSpec((n_e, LANES), lambda i: (0, 0))],
        out_specs=[tok_spec, tok_spec, tok_spec, pl.BlockSpec((n_e, LANES), lambda i: (0, 0))],
        out_shape=[tok(I32), tok(F32), tok(I32), jax.ShapeDtypeStruct((n_e, LANES), I32)],
        scratch_shapes=[pltpu.VMEM((n_e, LANES), F32)],
        compiler_params=_cparams(("arbitrary",), 32),
        name="moe_route",
    )(logits_t, b_router_lanes)


DMA_ISSUE_UNROLL = 8


def _dispatch_kernel(cnt_ref, pstart_ref, nu_ref, pos_ref, h2g_ref, xs_hbm, zero_ref, sem, zsem, gsem, *, tm, sub):
    i = pl.program_id(0)
    group_sub = MOE_GROUP_ROWS * sub

    @pl.when(i == 0)
    def _():
        zero_ref[...] = jnp.zeros(zero_ref.shape, zero_ref.dtype)
        zero_row = zero_ref.at[pl.ds(0, sub), :]

        def per_expert(e, n_zero):
            count = cnt_ref[e]
            padded = (count + MOE_GROUP_ROWS - 1) // MOE_GROUP_ROWS * MOE_GROUP_ROWS
            base = pstart_ref[e]

            def fill(r, carry):
                dst = xs_hbm.at[pl.ds(pl.multiple_of((base + r) * sub, sub), sub), :]
                pltpu.make_async_copy(zero_row, dst, zsem).start()
                return carry

            lax.fori_loop(count, padded, fill, 0)
            return n_zero + (padded - count)

        n_zero = lax.fori_loop(0, N_EXPERTS, per_expert, 0)
        n_groups = xs_hbm.shape[0] // group_sub

        def fill_group(g, carry):
            dst = xs_hbm.at[pl.ds(pl.multiple_of(g * group_sub, group_sub), group_sub), :]
            pltpu.make_async_copy(zero_ref, dst, gsem).start()
            return carry

        lax.fori_loop(nu_ref[0], n_groups, fill_group, 0)

        def drain(r, carry):
            pltpu.make_async_copy(zero_row, xs_hbm.at[pl.ds(0, sub), :], zsem).wait()
            return carry

        lax.fori_loop(0, n_zero, drain, 0)

        def drain_group(g, carry):
            pltpu.make_async_copy(zero_ref, xs_hbm.at[pl.ds(0, group_sub), :], gsem).wait()
            return carry

        lax.fori_loop(nu_ref[0], n_groups, drain_group, 0)

    def issue(tb, carry):
        for u in range(DMA_ISSUE_UNROLL):
            t = tb * DMA_ISSUE_UNROLL + u
            src = h2g_ref.at[pl.ds(pl.multiple_of(t * sub, sub), sub), :]
            for k in range(TOP_K):
                p = pl.multiple_of(pos_ref[k * tm + t], sub)
                pltpu.make_async_copy(src, xs_hbm.at[pl.ds(p, sub), :], sem).start(priority=k % 2)
        return carry

    lax.fori_loop(0, tm // DMA_ISSUE_UNROLL, issue, 0)
    for _ in range(TOP_K):
        pltpu.make_async_copy(h2g_ref, xs_hbm.at[pl.ds(0, tm * sub), :], sem).wait()


def _dispatch_call(counts, pad_start, n_used, pos_tiles, h2g, n_rows_sorted, sub):
    tm = ROW_TILE
    m = h2g.shape[0] // sub
    grid_spec = pltpu.PrefetchScalarGridSpec(
        num_scalar_prefetch=3,
        grid=(m // tm,),
        in_specs=[
            pl.BlockSpec((TOP_K * tm,), lambda i, c, p, u: (i,), memory_space=pltpu.SMEM),
            pl.BlockSpec((tm * sub, LANES), lambda i, c, p, u: (i, 0)),
        ],
        out_specs=pl.BlockSpec(memory_space=pl.ANY),
        scratch_shapes=[pltpu.VMEM((MOE_GROUP_ROWS * sub, LANES), h2g.dtype), pltpu.SemaphoreType.DMA(()),
                        pltpu.SemaphoreType.DMA(()), pltpu.SemaphoreType.DMA(())],
    )
    return pl.pallas_call(
        functools.partial(_dispatch_kernel, tm=tm, sub=sub),
        grid_spec=grid_spec,
        out_shape=jax.ShapeDtypeStruct((n_rows_sorted * sub, LANES), h2g.dtype),
        compiler_params=_cparams(("arbitrary",), 32),
        name="moe_dispatch",
    )(counts, pad_start, n_used, pos_tiles, h2g)


def _expert_kernel(ge_ref, nu_ref, x_ref, w1_ref, w3_ref, w2_ref, y_ref):
    g = pl.program_id(0)
    rows = MOE_GROUP_ROWS

    @pl.when(g < nu_ref[0])
    def _():
        xb = _unpack_bf16_pairs(_load_row_tiles(x_ref, rows, x_ref.shape[0] // rows))
        hidden = _silu(_dot(xb, w1_ref[0])) * _dot(xb, w3_ref[0])
        _store_row_tiles(y_ref, _dot(hidden.astype(BF16), w2_ref[0]))

    @pl.when(g >= nu_ref[0])
    def _():
        y_ref[...] = jnp.zeros(y_ref.shape, F32)


def _expert_call(group_expert, n_used, xs, w1, w3, w2):
    n_groups = group_expert.shape[0]
    rows = MOE_GROUP_ROWS
    _, d, hidden = w1.shape
    sub = d // LANES
    used = lambda g, nu: jnp.minimum(g, nu[0] - 1)
    grid_spec = pltpu.PrefetchScalarGridSpec(
        num_scalar_prefetch=2,
        grid=(n_groups,),
        in_specs=[
            pl.BlockSpec((xs.shape[0] // n_groups, LANES), lambda g, ge, nu: (used(g, nu), 0)),
            pl.BlockSpec((1, d, hidden), lambda g, ge, nu: (ge[used(g, nu)], 0, 0)),
            pl.BlockSpec((1, d, hidden), lambda g, ge, nu: (ge[used(g, nu)], 0, 0)),
            pl.BlockSpec((1, hidden, d), lambda g, ge, nu: (ge[used(g, nu)], 0, 0)),
        ],
        out_specs=pl.BlockSpec((rows * sub, LANES), lambda g, ge, nu: (g, 0)),
    )
    return pl.pallas_call(
        _expert_kernel,
        grid_spec=grid_spec,
        out_shape=jax.ShapeDtypeStruct((n_groups * rows * sub, LANES), F32),
        compiler_params=_cparams(("arbitrary",), 40),
        name="moe_experts",
    )(group_expert, n_used, xs, w1, w3, w2)


def _start_tile_gather(idx_ref, n_rows, sub, src_hbm, dst, sem):
    def issue(rb, carry):
        for u in range(DMA_ISSUE_UNROLL):
            r = rb * DMA_ISSUE_UNROLL + u
            p = pl.multiple_of(idx_ref[r], sub)
            pltpu.make_async_copy(src_hbm.at[pl.ds(p, sub), :],
                                  dst.at[pl.ds(pl.multiple_of(r * sub, sub), sub), :], sem).start(priority=u % 2)
        return carry

    lax.fori_loop(0, n_rows // DMA_ISSUE_UNROLL, issue, 0)


def _final_kernel(pos_ref, pos_next_ref, gw_ref, x1_ref, h2g_ref, g2_ref, nwf_ref, ws1_ref, ws3_ref, ws2_ref, ys_hbm,
                  y_ref, buf, sem):
    i = pl.program_id(0)
    n_i = pl.num_programs(0)
    g, r, d = x1_ref.shape
    tm = g * r
    sub = d // LANES
    n_rows = TOP_K * tm
    slot = i % 2

    @pl.when(i == 0)
    def _():
        _start_tile_gather(pos_ref, n_rows, sub, ys_hbm, buf.at[0], sem.at[0])

    @pl.when(i + 1 < n_i)
    def _():
        _start_tile_gather(pos_next_ref, n_rows, sub, ys_hbm, buf.at[1 - slot], sem.at[1 - slot])

    hb = _unpack_bf16_pairs(_load_row_tiles(h2g_ref, tm, h2g_ref.shape[0] // tm))
    shared = _dot((_silu(_dot(hb, ws1_ref[...])) * _dot(hb, ws3_ref[...])).astype(BF16), ws2_ref[...])
    gw_t = jnp.concatenate([gw_ref[...], jnp.zeros((LANES - TOP_K, tm), F32)], axis=0).T
    gates = [jnp.broadcast_to(gw_t[:, k:k + 1], (tm, LANES)) for k in range(TOP_K)]
    pltpu.make_async_copy(ys_hbm.at[pl.ds(0, n_rows * sub), :], buf.at[slot], sem.at[slot]).wait()
    parts = []
    for s in range(sub):
        acc = None
        for k in range(TOP_K):
            term = buf[slot, pl.ds(k * tm * sub + s, tm, stride=sub), :] * gates[k]
            acc = term if acc is None else acc + term
        parts.append(acc)
    routed = jnp.concatenate(parts, axis=1)
    x2 = x1_ref[...] + g2_ref[...] * (shared + routed).reshape(g, r, d)
    ms = jnp.mean(x2 * x2, axis=-1, keepdims=True)
    y_ref[...] = x2 * lax.rsqrt(ms + EPS) * nwf_ref[...]


def _final_call(pos_tiles, gw, x3, h2g, mod3, nwf, ws1, ws3, ws2, ysorted, *, groups_per_tile, mod_of_tile,
                first_tile, name):
    n_groups, r, d = x3.shape
    tm = groups_per_tile * r
    sub = d // LANES
    n_tiles = n_groups // groups_per_tile
    n_rows = TOP_K * tm
    const = lambda a: pl.BlockSpec(a.shape, lambda i: (0,) * a.ndim, pipeline_mode=pl.Buffered(1))
    return pl.pallas_call(
        _final_kernel,
        grid=(n_tiles,),
        in_specs=[
            pl.BlockSpec((n_rows,), lambda i: (first_tile + i,), memory_space=pltpu.SMEM),
            pl.BlockSpec((n_rows,), lambda i: (first_tile + jnp.minimum(i + 1, n_tiles - 1),),
                         memory_space=pltpu.SMEM),
            pl.BlockSpec((TOP_K, tm), lambda i: (0, first_tile + i)),
            pl.BlockSpec((groups_per_tile, r, d), lambda i: (i, 0, 0)),
            pl.BlockSpec((tm * (h2g.shape[0] * TOP_K // pos_tiles.shape[0]), LANES), lambda i: (first_tile + i, 0)),
            pl.BlockSpec((groups_per_tile, 1, d), lambda i: (mod_of_tile(i), 0, 5)),
            const(nwf), const(ws1), const(ws3), const(ws2),
            pl.BlockSpec(memory_space=pl.ANY),
        ],
        out_specs=pl.BlockSpec((groups_per_tile, r, d), lambda i: (i, 0, 0)),
        out_shape=jax.ShapeDtypeStruct((n_groups, r, d), F32),
        scratch_shapes=[pltpu.VMEM((2, n_rows * sub, LANES), F32), pltpu.SemaphoreType.DMA((2,))],
        compiler_params=_cparams(("arbitrary",), 56),
        name=name,
    )(pos_tiles, pos_tiles, gw, x3, h2g, mod3, nwf, ws1, ws3, ws2, ysorted)


def _paged_attn_kernel(pt_ref, ck_hbm, cv_hbm, q_ref, kn_ref, vn_ref, o_ref,
                       cbuf, sem, qbd_ref, s_ref, p_ref, l_ref, acc_ref, *, pages_per_seq, n_blocks, n_sel):
    n = pl.program_id(0)
    u = pl.program_id(1)
    n_seq = pl.num_programs(0)
    n_u = pl.num_programs(1)
    n_kc = n_u // 2
    _, pages_step, heads, hd, page = cbuf.shape
    width = heads * hd
    rows_step = pages_step * page
    blocks_step = rows_step // MOBA_BLOCK
    t_new = q_ref.shape[0]
    n_cols = heads * t_new
    slot = (n * n_u + u) % 2

    def start_chunk(nn, uu, s):
        def copy_pages(src_hbm, cc):
            for p in range(pages_step):
                page_id = pt_ref[nn * pages_per_seq + cc * pages_step + p]
                pltpu.make_async_copy(src_hbm.at[page_id], cbuf.at[s, p], sem.at[s]).start()

        @pl.when(uu < n_kc)
        def _():
            copy_pages(ck_hbm, uu)

        @pl.when(uu >= n_kc)
        def _():
            copy_pages(cv_hbm, uu - n_kc)

    @pl.when((n == 0) & (u == 0))
    def _():
        start_chunk(n, u, slot)

    last_u = u == n_u - 1

    @pl.when(jnp.logical_not(last_u & (n == n_seq - 1)))
    def _():
        start_chunk(jnp.where(last_u, n + 1, n), jnp.where(last_u, 0, u + 1), 1 - slot)

    @pl.when(u == 0)
    def _():
        q = q_ref[...] * (A_HEAD_DIM ** -0.5)
        qt = jnp.concatenate([q] * heads, axis=0)
        r_i = lax.broadcasted_iota(I32, qt.shape, 0)
        c_i = lax.broadcasted_iota(I32, qt.shape, 1)
        qbd = jnp.where(c_i // hd == r_i // t_new, qt, 0.0)
        hi = qbd.astype(BF16)
        qbd_ref[...] = jnp.concatenate([hi, (qbd - hi.astype(F32)).astype(BF16)], axis=0)

    def logits(k_t):
        s2 = _dot(qbd_ref[...], k_t)
        return s2[0:n_cols] + s2[n_cols:2 * n_cols]

    pltpu.make_async_copy(ck_hbm.at[pl.ds(0, pages_step)], cbuf.at[slot], sem.at[slot]).wait()
    chunk_t = jnp.concatenate([cbuf[slot, p].reshape(width, page) for p in range(pages_step)], axis=1).astype(BF16)

    @pl.when(u < n_kc)
    def _():
        s_ref[u] = logits(chunk_t)

    def block_logits(ref, j):
        c, off = divmod(j, blocks_step)
        return ref[c, :, off * MOBA_BLOCK:(off + 1) * MOBA_BLOCK]

    @pl.when(u == n_kc - 1)
    def _():
        lane = lax.broadcasted_iota(I32, (n_cols, LANES), 1)
        sc = jnp.full((n_cols, LANES), NEG, F32)
        for j in range(n_blocks):
            sc = jnp.where(lane == j, jnp.sum(block_logits(s_ref, j), axis=1, keepdims=True), sc)
        sel = jnp.zeros(sc.shape, jnp.bool_)
        for _ in range(n_sel):
            first = _first_argmax(sc, lane, 1, LANES)
            hit = lane == first
            sel = sel | hit
            sc = jnp.where(hit, BELOW_NEG, sc)
        keep = jnp.where(sel, 0.0, NEG)
        bias = [jnp.max(jnp.where(lane == j, keep, NEG), axis=1, keepdims=True) for j in range(n_blocks)]
        s_new = lax.dot_general(qbd_ref[...], kn_ref[...].astype(BF16), (((1,), (1,)), ((), ())),
                                preferred_element_type=F32)
        s_new = s_new[0:n_cols] + s_new[n_cols:2 * n_cols]
        r_i = lax.broadcasted_iota(I32, s_new.shape, 0)
        c_i = lax.broadcasted_iota(I32, s_new.shape, 1)
        s_new = jnp.where(c_i <= r_i % t_new, s_new, NEG)
        m = jnp.max(s_new, axis=1, keepdims=True)
        for j in range(n_blocks):
            m = jnp.maximum(m, jnp.max(block_logits(s_ref, j) + bias[j], axis=1, keepdims=True))
        p_new = jnp.exp(s_new - m)
        l = jnp.sum(p_new, axis=1, keepdims=True)
        for j in range(n_blocks):
            pj = jnp.exp(block_logits(s_ref, j) + bias[j] - m)
            l = l + jnp.sum(pj, axis=1, keepdims=True)
            c, off = divmod(j, blocks_step)
            p_ref[c, :, off * MOBA_BLOCK:(off + 1) * MOBA_BLOCK] = pj.astype(BF16)
        l_ref[...] = jnp.broadcast_to(l, l_ref.shape)
        acc_ref[...] = _dot(p_new.astype(BF16), vn_ref[...].astype(BF16))

    @pl.when(u >= n_kc)
    def _():
        acc_ref[...] += _dot_nt(p_ref[u - n_kc], chunk_t)

    @pl.when(last_u)
    def _():
        o_ref[0] = acc_ref[...] / l_ref[:, 0:1]


def _paged_attn_call(page_table_flat, cache_k4, cache_v4, q, k_new, v_new, n_seq, pages_per_seq, t_new):
    _, heads, hd, page = cache_k4.shape
    width = heads * hd
    past = pages_per_seq * page
    n_blocks = past // MOBA_BLOCK
    pages_step = PAGED_ATTN_PAGES_PER_STEP
    while pages_per_seq % pages_step:
        pages_step //= 2
    rows_step = pages_step * page
    n_cols = heads * t_new
    assert rows_step % MOBA_BLOCK == 0 and 2 * n_cols <= LANES and n_blocks <= LANES
    n_kc = pages_per_seq // pages_step
    n_sel = min(MOBA_TOPK, n_blocks)
    new_spec = pl.BlockSpec((t_new, width), lambda n, u, pt: (n, 0))
    grid_spec = pltpu.PrefetchScalarGridSpec(
        num_scalar_prefetch=1,
        grid=(n_seq, 2 * n_kc),
        in_specs=[pl.BlockSpec(memory_space=pl.ANY), pl.BlockSpec(memory_space=pl.ANY), new_spec, new_spec, new_spec],
        out_specs=pl.BlockSpec((1, n_cols, width), lambda n, u, pt: (n, 0, 0)),
        scratch_shapes=[
            pltpu.VMEM((2, pages_step, heads, hd, page), F32),
            pltpu.SemaphoreType.DMA((2,)),
            pltpu.VMEM((2 * n_cols, width), BF16),
            pltpu.VMEM((n_kc, n_cols, rows_step), F32),
            pltpu.VMEM((n_kc, n_cols, rows_step), BF16),
            pltpu.VMEM((n_cols, LANES), F32),
            pltpu.VMEM((n_cols, width), F32),
        ],
    )
    return pl.pallas_call(
        functools.partial(_paged_attn_kernel, pages_per_seq=pages_per_seq, n_blocks=n_blocks, n_sel=n_sel),
        grid_spec=grid_spec,
        out_shape=jax.ShapeDtypeStruct((n_seq, n_cols, width), F32),
        compiler_params=_cparams(("arbitrary", "arbitrary"), 56),
        name="sample_attention",
    )(page_table_flat, cache_k4, cache_v4, q, k_new, v_new)


def _rope_tables(pos):
    inv_freq = ROPE_THETA ** (-jnp.arange(0, A_HEAD_DIM, 2, dtype=F32) / A_HEAD_DIM)
    ang = pos.astype(F32)[:, None] * inv_freq[None, :]
    cos = jnp.concatenate([jnp.cos(ang)] * (2 * LANES // A_HEAD_DIM), axis=-1)
    sin = jnp.concatenate([jnp.sin(ang)] * (2 * LANES // A_HEAD_DIM), axis=-1)
    return cos, sin


def _dispatch_tables(idx, rank, counts):
    k, m = idx.shape
    rows_g = MOE_GROUP_ROWS
    padded = (counts + rows_g - 1) // rows_g * rows_g
    pad_end = jnp.cumsum(padded)
    pad_start = pad_end - padded
    n_groups = -(-(k * m) // rows_g) + N_EXPERTS
    onehot = idx[:, :, None] == jnp.arange(N_EXPERTS, dtype=I32)[None, None, :]
    pos = jnp.sum(jnp.where(onehot, pad_start[None, None, :], 0), axis=-1) + rank
    group_start = jnp.arange(n_groups, dtype=I32) * rows_g
    group_expert = jnp.minimum(jnp.sum(pad_end[None, :] <= group_start[:, None], axis=1), N_EXPERTS - 1).astype(I32)
    n_used = (pad_end[-1:] // rows_g).astype(I32)
    pos_tiles = pos.reshape(k, m // ROW_TILE, ROW_TILE).transpose(1, 0, 2).reshape(-1).astype(I32)
    return group_expert, n_used, pad_start.astype(I32), pos_tiles, n_groups


def kernel(x_prompt, x_sample, cache_k, cache_v, state_gla, page_table, c_prompt, c_sample, w_ada, b_ada,
           norm_mix_w, norm_ffn_w, norm_final_w, w_in, w_gate_up, b_gate, gla_norm_w, w_proj_a, w_proj_b, w_out,
           w_router, b_router, w_e1, w_e3, w_e2, w_s1, w_s3, w_s2):
    batch, seq, d = x_prompt.shape
    n_dec, t_new, _ = x_sample.shape
    depth = w_ada.shape[0]
    n_pool, page = cache_k.shape[1], cache_k.shape[2]
    pages_per_seq = page_table.shape[1]
    past_len = pages_per_seq * page
    assert depth == 1 and past_len % MOBA_BLOCK == 0 and MOBA_BLOCK % page == 0
    assert seq % ROW_TILE == 0 and ROW_TILE % MOBA_BLOCK == 0 and ROW_TILE % t_new == 0 and t_new == SUBLANES
    assert seq % ATTN_KEY_CHUNK == 0 and ATTN_KEY_CHUNK % MOBA_BLOCK == 0 and (m_all := batch * seq + n_dec * t_new) % ROW_TILE == 0
    del m_all
    m_p, m_s = batch * seq, n_dec * t_new
    bkw = w_gate_up.shape[2]
    dk = bkw // B_HEADS
    dv = gla_norm_w.shape[1]
    bvw = B_HEADS * dv

    n_c = batch + n_dec
    n_c_pad = -(-n_c // SUBLANES) * SUBLANES
    c_all = jnp.concatenate([c_prompt, c_sample, jnp.zeros((n_c_pad - n_c, d), F32)], axis=0)
    mod = _ada_call(c_all, w_ada[0], b_ada[0])
    mod_p = mod[:batch].reshape(batch, 1, 6 * d)
    mod_s = mod[batch:n_c].reshape(n_dec, 1, 6 * d)

    sizes = (A_WIDTH, A_WIDTH, A_WIDTH, bkw, bkw, bvw, bvw, GATE_RANK, d, d)
    offs = [0]
    for s in sizes:
        offs.append(offs[-1] + s)
    w_in0 = w_in[0]
    w_main = jnp.concatenate([w_in0[:, :offs[7]], w_in0[:, offs[8]:]], axis=1).astype(BF16)
    w_gr = jnp.pad(w_in0[:, offs[7]:offs[8]], ((0, 0), (0, LANES - GATE_RANK))).astype(BF16)
    wg = jnp.pad(w_gate_up[0], ((0, LANES - GATE_RANK), (0, 0)))
    bg = b_gate[0].reshape(1, bkw)
    nw1 = norm_mix_w[0].reshape(1, d)
    nw2 = norm_ffn_w[0].reshape(1, d)
    nwf = norm_final_w.reshape(1, d)
    gnw = gla_norm_w[0].reshape(1, dv)
    wpa = w_proj_a[0].astype(BF16)
    wpb = w_proj_b[0].astype(BF16)
    wout = w_out[0].astype(BF16)
    wr_t = w_router[0].T
    br_lanes = jnp.broadcast_to(b_router[0][:, None], (N_EXPERTS, LANES))
    we1, we3, we2 = w_e1[0].astype(BF16), w_e3[0].astype(BF16), w_e2[0].astype(BF16)
    ws1, ws3, ws2 = w_s1[0].astype(BF16), w_s3[0].astype(BF16), w_s2[0].astype(BF16)

    tiles_per_seq = seq // ROW_TILE
    groups_s = ROW_TILE // t_new

    cos_p, sin_p = _rope_tables(jnp.arange(seq, dtype=I32))
    xp3 = x_prompt.reshape(m_p // ROW_TILE, ROW_TILE, d)
    (qa_p, ka_p, va_p, qb_p, kb_p, vb_p, gb_p, ga_p, gm_p, gr_p, kmean_p) = _inproj_call(
        xp3, mod_p, nw1, cos_p, sin_p, w_main, w_gr, groups_per_tile=1, rows_per_group=ROW_TILE,
        mod_of_tile=lambda i: i // tiles_per_seq, cos_of_tile=lambda i: i % tiles_per_seq, with_kmean=True,
        name="inproj_prompt")
    qaug, kaug, vaug = _select_call(qa_p, ka_p, va_p, kmean_p, batch, seq)
    oa_p = _moba_attn_call(qaug, kaug, vaug).reshape(m_p, A_WIDTH)
    ob_p, state_p = _gla_call(qb_p, kb_p, vb_p, gr_p, wg, bg, None, n_seq=batch, seq_len=seq, chunk=GLA_CHUNK,
                              chunks_per_step=GLA_CHUNKS_PER_STEP, name="gla_prompt")
    x1_p, h2_p, lg_p = _mixer_call(
        xp3, oa_p, ob_p, gb_p, ga_p, gm_p, mod_p, nw2, gnw, wpa, wpb, wout, wr_t, groups_per_tile=1,
        mod_of_tile=lambda i: i // tiles_per_seq, name="mixer_prompt")

    cos_s, sin_s = _rope_tables(past_len + jnp.arange(t_new, dtype=I32))
    cos_s = jnp.tile(cos_s, (groups_s, 1))
    sin_s = jnp.tile(sin_s, (groups_s, 1))
    (qa_s, ka_s, va_s, qb_s, kb_s, vb_s, gb_s, ga_s, gm_s, gr_s) = _inproj_call(
        x_sample, mod_s, nw1, cos_s, sin_s, w_main, w_gr, groups_per_tile=groups_s, rows_per_group=t_new,
        mod_of_tile=lambda i: i, cos_of_tile=lambda i: 0, with_kmean=False, name="inproj_sample")
    pt_flat = page_table.reshape(-1)
    ck4 = cache_k[0].transpose(0, 2, 3, 1)
    cv4 = cache_v[0].transpose(0, 2, 3, 1)
    o_c = _paged_attn_call(pt_flat, ck4, cv4, qa_s, ka_s, va_s, n_dec, pages_per_seq, t_new)
    o5 = o_c.reshape(n_dec, A_HEADS, t_new, A_HEADS, A_HEAD_DIM)
    hh = jnp.arange(A_HEADS)
    oa_s = o5[:, hh, :, hh, :].transpose(1, 2, 0, 3).reshape(m_s, A_WIDTH)
    ob_s, state_s = _gla_call(qb_s, kb_s, vb_s, gr_s, wg, bg, state_gla[0], n_seq=n_dec, seq_len=t_new, chunk=t_new,
                              chunks_per_step=1, name="gla_sample")
    x1_s, h2_s, lg_s = _mixer_call(
        x_sample, oa_s, ob_s, gb_s, ga_s, gm_s, mod_s, nw2, gnw, wpa, wpb, wout, wr_t, groups_per_tile=groups_s,
        mod_of_tile=lambda i: i, name="mixer_sample")

    sub_x = d // (2 * LANES)
    sub_y = d // LANES
    h2g = jnp.concatenate([h2_p, h2_s], axis=0)
    idx, gw, rank, cnt = _route_call(jnp.concatenate([lg_p, lg_s], axis=1), br_lanes)
    counts = cnt[:, 0]
    group_expert, n_used, pad_start, pos_tiles, n_groups = _dispatch_tables(idx, rank, counts)
    xsorted = _dispatch_call(counts, pad_start, n_used, pos_tiles * sub_x, h2g, n_groups * MOE_GROUP_ROWS, sub_x)
    ysorted = _expert_call(group_expert, n_used, xsorted, we1, we3, we2)
    n_tiles_p = m_p // ROW_TILE
    pos_y = pos_tiles * sub_y
    y_p = _final_call(pos_y, gw, x1_p, h2g, mod_p, nwf, ws1, ws3, ws2, ysorted, groups_per_tile=1,
                      mod_of_tile=lambda i: i // tiles_per_seq, first_tile=0, name="final_prompt")
    y_s = _final_call(pos_y, gw, x1_s, h2g, mod_s, nwf, ws1, ws3, ws2, ysorted, groups_per_tile=groups_s,
                      mod_of_tile=lambda i: i, first_tile=n_tiles_p, name="final_sample")

    return (
        y_p.reshape(batch, seq, d),
        y_s,
        ka_p.reshape(1, batch, seq, A_HEADS, A_HEAD_DIM),
        va_p.reshape(1, batch, seq, A_HEADS, A_HEAD_DIM),
        state_p[None],
        ka_s.reshape(1, n_dec, t_new, A_HEADS, A_HEAD_DIM),
        va_s.reshape(1, n_dec, t_new, A_HEADS, A_HEAD_DIM),
        state_s[None],
    )
```

```python
import functools

import jax
import jax.numpy as jnp
from jax import lax
from jax.experimental import pallas as pl
from jax.experimental.pallas import tpu as pltpu

F32 = jnp.float32
BF16 = jnp.bfloat16
I32 = jnp.int32
HIGHEST = lax.Precision.HIGHEST

A_HEADS = 8
A_HEAD_DIM = 64
A_WIDTH = A_HEADS * A_HEAD_DIM
MOBA_BLOCK = 256
MOBA_TOPK = 3
ROPE_THETA = 10000.0
B_HEADS = 4
GATE_RANK = 16
GATE_NORMALIZER = 16.0
N_EXPERTS = 64
N_GROUPS = 8
GROUP_SIZE = N_EXPERTS // N_GROUPS
TOPK_GROUPS = 4
TOP_K = 8
ROUTED_SCALE = 2.5
EPS = 1e-6
NEG = -1e30
BELOW_NEG = -3e38

LANES = 128
SUBLANES = 8
VMEM_BYTES = 64 * 1024 * 1024

LOG2_E = 1.4426950408889634
ATTN_KEY_CHUNK = 2048
ATTN_HEADS_PER_STEP = 2

PAGED_ATTN_PAGES_PER_STEP = 32

ROW_TILE = 256
GLA_CHUNK = 64
GLA_CHUNKS_PER_STEP = 4
MOE_GROUP_ROWS = 512


def _cparams(semantics, vmem_mb):
    return pltpu.CompilerParams(dimension_semantics=semantics, vmem_limit_bytes=vmem_mb * 1024 * 1024)


def _silu(x):
    return x * jax.nn.sigmoid(x)


def _dot(a, b):
    return jnp.dot(a, b, preferred_element_type=F32)


def _dot_nt(a, b, precision=None):
    return lax.dot_general(a, b, (((1,), (1,)), ((), ())), precision=precision, preferred_element_type=F32)


def _dot_tn(a, b):
    return lax.dot_general(a, b, (((0,), (0,)), ((), ())), preferred_element_type=F32)


def _first_argmax(vals, iota, axis, size):
    mx = jnp.max(vals, axis=axis, keepdims=True)
    return jnp.min(jnp.where(vals == mx, iota, size), axis=axis, keepdims=True)


def _store_row_tiles(ref, x):
    rows, d = x.shape
    sub = d // LANES
    for s in range(sub):
        ref[pl.ds(s, rows, stride=sub), :] = x[:, s * LANES:(s + 1) * LANES]


def _load_row_tiles(ref, rows, sub):
    return jnp.concatenate([ref[pl.ds(s, rows, stride=sub), :] for s in range(sub)], axis=1)


def _pack_bf16_pairs(x):
    half = x.shape[1] // 2
    hi = lax.bitcast_convert_type(x[:, :half].astype(BF16).astype(F32), jnp.uint32)
    lo = lax.bitcast_convert_type(x[:, half:].astype(BF16).astype(F32), jnp.uint32)
    return hi | (lo >> 16)


def _unpack_bf16_pairs(w):
    hi = lax.bitcast_convert_type(w & jnp.uint32(0xFFFF0000), F32)
    lo = lax.bitcast_convert_type(w << 16, F32)
    return jnp.concatenate([hi, lo], axis=1).astype(BF16)


def _ada_kernel(c_ref, w_ref, b_ref, o_ref):
    s = _silu(c_ref[...])
    o_ref[...] = jnp.dot(s, w_ref[...], precision=HIGHEST, preferred_element_type=F32) + b_ref[...]


def _ada_call(c_all, w_ada, b_ada):
    n, d = c_all.shape
    width = w_ada.shape[1]
    tn = width // 4
    return pl.pallas_call(
        _ada_kernel,
        grid=(width // tn,),
        in_specs=[
            pl.BlockSpec((n, d), lambda j: (0, 0)),
            pl.BlockSpec((d, tn), lambda j: (0, j)),
            pl.BlockSpec((1, tn), lambda j: (0, j)),
        ],
        out_specs=pl.BlockSpec((n, tn), lambda j: (0, j)),
        out_shape=jax.ShapeDtypeStruct((n, width), F32),
        compiler_params=_cparams(("arbitrary",), 40),
        name="adaln",
    )(c_all, w_ada, b_ada.reshape(1, width))


def _rope(a, cos, sin):
    lane = lax.broadcasted_iota(I32, cos.shape, 1)
    first_half = (lane % A_HEAD_DIM) < (A_HEAD_DIM // 2)
    outs = []
    for c in range(a.shape[1] // LANES):
        xc = a[:, c * LANES:(c + 1) * LANES]
        rot = jnp.where(first_half, -pltpu.roll(xc, LANES - A_HEAD_DIM // 2, 1), pltpu.roll(xc, A_HEAD_DIM // 2, 1))
        outs.append(xc * cos + rot * sin)
    return jnp.concatenate(outs, axis=1)


def _modulated_norm(x, norm_w, scale, shift):
    ms = jnp.mean(x * x, axis=-1, keepdims=True)
    y = x * lax.rsqrt(ms + EPS) * norm_w
    return y * (1.0 + scale) + shift


def _inproj_kernel(x_ref, sh_ref, sc_ref, nw_ref, cos_ref, sin_ref, w_ref, wgr_ref,
                   q_ref, k_ref, v_ref, qb_ref, kb_ref, vb_ref, gb_ref, ga_ref, gm_ref, gr_ref, *km_refs):
    g, r, d = x_ref.shape
    h = _modulated_norm(x_ref[...], nw_ref[...], sc_ref[...], sh_ref[...])
    hb = h.reshape(g * r, d).astype(BF16)
    cos = cos_ref[...]
    sin = sin_ref[...]
    col = 0

    def proj(width):
        nonlocal col
        out = _dot(hb, w_ref[:, col:col + width])
        col += width
        return out

    q_ref[...] = _rope(proj(A_WIDTH), cos, sin)
    k = _rope(proj(A_WIDTH), cos, sin)
    k_ref[...] = k
    if km_refs:
        nb = (g * r) // MOBA_BLOCK
        km_refs[0][...] = jnp.mean(k.reshape(nb, MOBA_BLOCK, A_WIDTH), axis=1, keepdims=True)
    v_ref[...] = proj(A_WIDTH)
    for ref in (qb_ref, kb_ref, vb_ref, gb_ref, ga_ref, gm_ref):
        ref[...] = proj(ref.shape[1])
    gr_ref[...] = _dot(hb, wgr_ref[...])


def _inproj_call(x3, mod3, norm_w, cos, sin, w_main, w_gr, *, groups_per_tile, rows_per_group, mod_of_tile,
                 cos_of_tile, with_kmean, name):
    n_groups, r, d = x3.shape
    assert r == rows_per_group
    m = n_groups * r
    tm = groups_per_tile * r
    n_tiles = n_groups // groups_per_tile
    d_model = d
    bk = w_main.shape[1] - 3 * A_WIDTH - 4 * d_model
    widths = [A_WIDTH, A_WIDTH, A_WIDTH, bk // 2, bk // 2, d_model, d_model, d_model, d_model, LANES]
    out_shape = [jax.ShapeDtypeStruct((m, w), F32) for w in widths]
    out_specs = [pl.BlockSpec((tm, w), lambda i: (i, 0)) for w in widths]
    if with_kmean:
        out_shape.append(jax.ShapeDtypeStruct((m // MOBA_BLOCK, 1, A_WIDTH), F32))
        out_specs.append(pl.BlockSpec((tm // MOBA_BLOCK, 1, A_WIDTH), lambda i: (i, 0, 0)))
    resident = dict(pipeline_mode=pl.Buffered(1))
    return pl.pallas_call(
        _inproj_kernel,
        grid=(n_tiles,),
        in_specs=[
            pl.BlockSpec((groups_per_tile, r, d), lambda i: (i, 0, 0)),
            pl.BlockSpec((groups_per_tile, 1, d), lambda i: (mod_of_tile(i), 0, 0)),
            pl.BlockSpec((groups_per_tile, 1, d), lambda i: (mod_of_tile(i), 0, 1)),
            pl.BlockSpec((1, d), lambda i: (0, 0), **resident),
            pl.BlockSpec((tm, LANES), lambda i: (cos_of_tile(i), 0)),
            pl.BlockSpec((tm, LANES), lambda i: (cos_of_tile(i), 0)),
            pl.BlockSpec(w_main.shape, lambda i: (0, 0), **resident),
            pl.BlockSpec(w_gr.shape, lambda i: (0, 0), **resident),
        ],
        out_specs=out_specs,
        out_shape=out_shape,
        compiler_params=_cparams(("parallel",), 48),
        name=name,
    )(x3, mod3, mod3, norm_w, cos, sin, w_main, w_gr)


def _select_kernel(q_ref, k_ref, v_ref, km_ref, qa_ref, ka_ref, va_ref, *, nbw):
    i = pl.program_id(1)
    q = q_ref[...]
    k = k_ref[...]
    v = v_ref[...]
    km = km_ref[:, 0, :]
    nblk = km.shape[0]
    rows = q.shape[0]
    q_t = q.T
    v_t = v.T
    blk = lax.broadcasted_iota(I32, (nblk, rows), 0)
    colw = lax.broadcasted_iota(I32, (rows, nbw), 1)
    ones_row = (lax.broadcasted_iota(I32, (A_HEAD_DIM, rows), 0) == 0).astype(F32)
    own_onehot = jnp.where(colw == i, -NEG, 0.0)
    for h in range(A_HEADS):
        sl = slice(h * A_HEAD_DIM, (h + 1) * A_HEAD_DIM)
        s = _dot_nt(km[:, sl], q[:, sl], precision=HIGHEST)
        s = jnp.where(blk < i, s, NEG)
        sel = jnp.zeros(s.shape, jnp.bool_)
        for _ in range(MOBA_TOPK):
            first = _first_argmax(s, blk, 0, nblk)
            hit = blk == first
            sel = sel | (hit & (first < i))
            s = jnp.where(hit, BELOW_NEG, s)
        selneg = jnp.where(sel, 0.0, -1.0)
        if nbw > nblk:
            selneg = jnp.concatenate([selneg, jnp.full((nbw - nblk, rows), -1.0, F32)], axis=0)
        qa_ref[0, h] = jnp.concatenate([q_t[sl, :] * (A_HEAD_DIM ** -0.5 * LOG2_E), selneg], axis=0).astype(BF16)
        ka_ref[0, h] = jnp.concatenate([k[:, sl], own_onehot], axis=1).astype(BF16)
        va_ref[0, h] = jnp.concatenate([v_t[sl, :], ones_row], axis=0).astype(BF16)


def _select_call(q, k, v, kmean, batch, seq):
    nblk = seq // MOBA_BLOCK
    nbw = -(-nblk // A_HEAD_DIM) * A_HEAD_DIM
    aw = A_HEAD_DIM + nbw
    row_spec = pl.BlockSpec((MOBA_BLOCK, A_WIDTH), lambda b, i: (b * nblk + i, 0))
    t_spec = lambda w: pl.BlockSpec((1, A_HEADS, w, MOBA_BLOCK), lambda b, i: (b, 0, 0, i))
    return pl.pallas_call(
        functools.partial(_select_kernel, nbw=nbw),
        grid=(batch, nblk),
        in_specs=[row_spec, row_spec, row_spec, pl.BlockSpec((nblk, 1, A_WIDTH), lambda b, i: (b, 0, 0))],
        out_specs=[t_spec(aw), pl.BlockSpec((1, A_HEADS, MOBA_BLOCK, aw), lambda b, i: (b, 0, i, 0)),
                   t_spec(2 * A_HEAD_DIM)],
        out_shape=[
            jax.ShapeDtypeStruct((batch, A_HEADS, aw, seq), BF16),
            jax.ShapeDtypeStruct((batch, A_HEADS, seq, aw), BF16),
            jax.ShapeDtypeStruct((batch, A_HEADS, 2 * A_HEAD_DIM, seq), BF16),
        ],
        compiler_params=_cparams(("parallel", "parallel"), 32),
        name="moba_select",
    )(q, k, v, kmean)


def _moba_attn_kernel(qa_ref, ka_ref, va_ref, kown_ref, vown_ref, o_ref, sa_ref, sb_ref, mxa_ref, mxb_ref,
                      m_ref, acc_ref):
    i = pl.program_id(2)
    chunk = ATTN_KEY_CHUNK
    heads = range(qa_ref.shape[1])

    def col_max(s_t):
        while s_t.shape[0] > SUBLANES:
            half = s_t.shape[0] // 2
            s_t = jnp.maximum(s_t[:half], s_t[half:])
        return s_t

    n_trips = (i * MOBA_BLOCK + chunk - 1) // chunk
    last_chunk = ka_ref.shape[2] // chunk - 1

    def stage_logits(buf, h, c):
        r0 = pl.multiple_of(jnp.minimum(c, last_chunk) * chunk, chunk)
        s_t = _dot(ka_ref[0, h, pl.ds(r0, chunk), :], qa_ref[0, h])
        buf[0][h] = s_t
        buf[1][h] = col_max(s_t)

    def consume(buf, h, c):
        r0 = pl.multiple_of(c * chunk, chunk)
        m_prev = m_ref[h, 0:1, :]
        m_new = jnp.maximum(m_prev, jnp.max(buf[1][h], axis=0, keepdims=True))
        alpha = jnp.exp2(m_prev - m_new)
        p_t = jnp.exp2(buf[0][h] - m_new).astype(BF16)
        acc_ref[h] = alpha * acc_ref[h] + _dot(va_ref[0, h, :, pl.ds(r0, chunk)], p_t)
        m_ref[h] = jnp.broadcast_to(m_new, m_ref.shape[1:])

    buf_a, buf_b = (sa_ref, mxa_ref), (sb_ref, mxb_ref)
    for h in heads:
        stage_logits(buf_a, h, 0)
    for h in heads:
        s_t = _dot(kown_ref[0, h][:, 0:A_HEAD_DIM], qa_ref[0, h][0:A_HEAD_DIM, :])
        key = lax.broadcasted_iota(I32, s_t.shape, 0)
        qry = lax.broadcasted_iota(I32, s_t.shape, 1)
        s_t = jnp.where(key <= qry, s_t, NEG)
        m_own = jnp.max(col_max(s_t), axis=0, keepdims=True)
        m_ref[h] = jnp.broadcast_to(m_own, m_ref.shape[1:])
        acc_ref[h] = _dot(vown_ref[0, h], jnp.exp2(s_t - m_own).astype(BF16))

    def past_chunk_pair(t, carry):
        c = 2 * t
        for h in heads:
            stage_logits(buf_b, h, c + 1)
        for h in heads:
            consume(buf_a, h, c)

        @pl.when(c + 1 < n_trips)
        def _():
            for h in heads:
                stage_logits(buf_a, h, c + 2)
            for h in heads:
                consume(buf_b, h, c + 1)

        return carry

    lax.fori_loop(0, (n_trips + 1) // 2, past_chunk_pair, 0)
    outs = []
    for h in heads:
        acc = acc_ref[h].T
        outs.append(acc[:, :A_HEAD_DIM] / acc[:, A_HEAD_DIM:A_HEAD_DIM + 1])
    o_ref[0] = jnp.concatenate(outs, axis=1)


def _moba_attn_call(qaug_t, kaug, vaug_t):
    batch, heads, seq, aw = kaug.shape
    nblk = seq // MOBA_BLOCK
    vw = vaug_t.shape[2]
    hs = ATTN_HEADS_PER_STEP
    return pl.pallas_call(
        _moba_attn_kernel,
        grid=(batch, heads // hs, nblk),
        in_specs=[
            pl.BlockSpec((1, hs, aw, MOBA_BLOCK), lambda b, h, i: (b, h, 0, i)),
            pl.BlockSpec((1, hs, seq, aw), lambda b, h, i: (b, h, 0, 0)),
            pl.BlockSpec((1, hs, vw, seq), lambda b, h, i: (b, h, 0, 0)),
            pl.BlockSpec((1, hs, MOBA_BLOCK, aw), lambda b, h, i: (b, h, i, 0)),
            pl.BlockSpec((1, hs, vw, MOBA_BLOCK), lambda b, h, i: (b, h, 0, i)),
        ],
        out_specs=pl.BlockSpec((1, MOBA_BLOCK, hs * A_HEAD_DIM), lambda b, h, i: (b, i, h)),
        out_shape=jax.ShapeDtypeStruct((batch, seq, heads * A_HEAD_DIM), F32),
        scratch_shapes=[pltpu.VMEM((hs, ATTN_KEY_CHUNK, MOBA_BLOCK), F32),
                        pltpu.VMEM((hs, ATTN_KEY_CHUNK, MOBA_BLOCK), F32),
                        pltpu.VMEM((hs, SUBLANES, MOBA_BLOCK), F32),
                        pltpu.VMEM((hs, SUBLANES, MOBA_BLOCK), F32),
                        pltpu.VMEM((hs, SUBLANES, MOBA_BLOCK), F32), pltpu.VMEM((hs, vw, MOBA_BLOCK), F32)],
        compiler_params=_cparams(("parallel", "parallel", "arbitrary"), 56),
        name="moba_attention",
    )(qaug_t, kaug, vaug_t, kaug, vaug_t)


def _gla_level_masks(c, dk):
    row = lax.broadcasted_iota(I32, (c, c), 0)
    col = lax.broadcasted_iota(I32, (c, c), 1)
    r1 = lax.broadcasted_iota(I32, (c, dk), 0)
    levels = []
    half = c // 2
    while half >= 1:
        grp = 2 * half
        valid = (row // grp == col // grp) & (row % grp >= half) & (col % grp < half)
        picks = {} if grp >= SUBLANES else {
            off: r1 % grp == half - 1 + off for off in range(-(half - 1), half + 1) if off != 0}
        levels.append((half, valid, picks))
        half //= 2
    return row == col, levels


def _gla_intra_scores(q, k, b, masks):
    c, dk = q.shape
    diagonal, levels = masks
    scores = jnp.where(diagonal, _dot_nt(q.astype(BF16), k.astype(BF16)), 0.0)
    for half, valid, picks in levels:
        grp = 2 * half
        if grp >= SUBLANES:
            a = jnp.concatenate([jnp.broadcast_to(b[j * grp + half - 1:j * grp + half, :], (grp, dk))
                                 for j in range(c // grp)], axis=0)
        else:
            a = b
            for off, pick in picks.items():
                a = jnp.where(pick, pltpu.roll(b, off % c, 0), a)
        qf = (q * jnp.exp(jnp.minimum(b - a, 0.0))).astype(BF16)
        kf = (k * jnp.exp(jnp.minimum(a - b, 0.0))).astype(BF16)
        scores = jnp.where(valid, _dot_nt(qf, kf), scores)
    return scores


def _gla_kernel(*refs, chunk, n_chunks, has_s0):
    if has_s0:
        q_ref, k_ref, v_ref, gr_ref, wg_ref, bg_ref, s0_ref, o_ref, sout_ref, st_ref = refs
    else:
        q_ref, k_ref, v_ref, gr_ref, wg_ref, bg_ref, o_ref, sout_ref, st_ref = refs
    step = pl.program_id(1)
    heads, dv, dk = st_ref.shape

    @pl.when(step == 0)
    def _():
        for h in range(heads):
            if has_s0:
                st_ref[h] = s0_ref[0, h].T
            else:
                st_ref[h] = jnp.zeros((dv, dk), F32)

    tri = (lax.broadcasted_iota(I32, (chunk, chunk), 1) <= lax.broadcasted_iota(I32, (chunk, chunk), 0)).astype(F32)
    masks = _gla_level_masks(chunk, dk)
    for c in range(n_chunks):
        sl = slice(c * chunk, (c + 1) * chunk)
        z = jnp.dot(gr_ref[sl, :], wg_ref[...], precision=HIGHEST, preferred_element_type=F32) + bg_ref[...]
        log_a = -(jnp.maximum(-z, 0.0) + jnp.log1p(jnp.exp(-jnp.abs(z)))) * (1.0 / GATE_NORMALIZER)
        b_all = jnp.dot(tri, log_a, precision=HIGHEST, preferred_element_type=F32)
        for h in range(heads):
            kl = slice(h * dk, (h + 1) * dk)
            vl = slice(h * dv, (h + 1) * dv)
            q = q_ref[sl, kl] * (dk ** -0.5)
            k = k_ref[sl, kl]
            v = v_ref[sl, vl].astype(BF16)
            b = b_all[:, kl]
            b_last = b[chunk - 1:chunk, :]
            st = st_ref[h]
            scores = _gla_intra_scores(q, k, b, masks)
            inter = _dot_nt((q * jnp.exp(b)).astype(BF16), st.astype(BF16))
            o_ref[sl, vl] = _dot(scores.astype(BF16), v) + inter
            kd = (k * jnp.exp(b_last - b)).astype(BF16)
            st_ref[h] = st * jnp.exp(b_last) + _dot_tn(v, kd)

    @pl.when(step == pl.num_programs(1) - 1)
    def _():
        for h in range(heads):
            sout_ref[0, h] = st_ref[h].T


def _gla_call(qb, kb, vb, gr, wg, bg, s0, *, n_seq, seq_len, chunk, chunks_per_step, name):
    heads = B_HEADS
    kw, vw = qb.shape[1], vb.shape[1]
    dk, dv = kw // heads, vw // heads
    rb = chunk * chunks_per_step
    steps = seq_len // rb
    rows = lambda w: pl.BlockSpec((rb, w), lambda n, s: (n * steps + s, 0))
    state = pl.BlockSpec((1, heads, dk, dv), lambda n, s: (n, 0, 0, 0))
    in_specs = [rows(kw), rows(kw), rows(vw), rows(LANES),
                pl.BlockSpec(wg.shape, lambda n, s: (0, 0)), pl.BlockSpec(bg.shape, lambda n, s: (0, 0))]
    args = [qb, kb, vb, gr, wg, bg]
    if s0 is not None:
        in_specs.append(state)
        args.append(s0)
    return pl.pallas_call(
        functools.partial(_gla_kernel, chunk=chunk, n_chunks=chunks_per_step, has_s0=s0 is not None),
        grid=(n_seq, steps),
        in_specs=in_specs,
        out_specs=[rows(vw), state],
        out_shape=[
            jax.ShapeDtypeStruct((n_seq * seq_len, vw), F32),
            jax.ShapeDtypeStruct((n_seq, heads, dk, dv), F32),
        ],
        scratch_shapes=[pltpu.VMEM((heads, dv, dk), F32)],
        compiler_params=_cparams(("parallel", "arbitrary"), 40),
        name=name,
    )(*args)


def _mixer_kernel(x_ref, oa_ref, ob_ref, gb_ref, ga_ref, gm_ref, g1_ref, sh2_ref, sc2_ref, nw2_ref, gnw_ref,
                  wpa_ref, wpb_ref, wout_ref, wr_ref, x1_ref, h2_ref, lg_ref):
    g, r, d = x_ref.shape
    tm = g * r
    dv = gnw_ref.shape[1]
    ob = ob_ref[...]
    gb = gb_ref[...]
    gnw = gnw_ref[...]
    parts = []
    for h in range(ob.shape[1] // dv):
        oh = ob[:, h * dv:(h + 1) * dv]
        ms = jnp.mean(oh * oh, axis=-1, keepdims=True)
        parts.append((oh * lax.rsqrt(ms + EPS) * gnw * _silu(gb[:, h * dv:(h + 1) * dv])).astype(BF16))
    pb = _dot(jnp.concatenate(parts, axis=1), wpb_ref[...])
    pa = _dot(oa_ref[...].astype(BF16), wpa_ref[...])
    merged = jax.nn.sigmoid(ga_ref[...]) * pa + jax.nn.sigmoid(gm_ref[...]) * pb
    u = _dot(merged.astype(BF16), wout_ref[...])
    x1 = x_ref[...] + g1_ref[...] * u.reshape(g, r, d)
    x1_ref[...] = x1
    h2 = _modulated_norm(x1, nw2_ref[...], sc2_ref[...], sh2_ref[...]).reshape(tm, d)
    _store_row_tiles(h2_ref, _pack_bf16_pairs(h2))
    lg_ref[...] = _dot_nt(wr_ref[...], h2, precision=HIGHEST)


def _mixer_call(x3, oa, ob, gb, ga, gm, mod3, nw2, gnw, wpa, wpb, wout, wr_t, *, groups_per_tile, mod_of_tile, name):
    n_groups, r, d = x3.shape
    m = n_groups * r
    tm = groups_per_tile * r
    n_tiles = n_groups // groups_per_tile
    row = lambda w: pl.BlockSpec((tm, w), lambda i: (i, 0))
    modspec = lambda k: pl.BlockSpec((groups_per_tile, 1, d), lambda i: (mod_of_tile(i), 0, k))
    const = lambda a: pl.BlockSpec(a.shape, lambda i: (0,) * a.ndim, pipeline_mode=pl.Buffered(1))
    return pl.pallas_call(
        _mixer_kernel,
        grid=(n_tiles,),
        in_specs=[
            pl.BlockSpec((groups_per_tile, r, d), lambda i: (i, 0, 0)),
            row(oa.shape[1]), row(ob.shape[1]), row(gb.shape[1]), row(d), row(d),
            modspec(2), modspec(3), modspec(4),
            const(nw2), const(gnw), const(wpa), const(wpb), const(wout), const(wr_t),
        ],
        out_specs=[
            pl.BlockSpec((groups_per_tile, r, d), lambda i: (i, 0, 0)),
            pl.BlockSpec((tm * d // (2 * LANES), LANES), lambda i: (i, 0)),
            pl.BlockSpec((N_EXPERTS, tm), lambda i: (0, i)),
        ],
        out_shape=[
            jax.ShapeDtypeStruct((n_groups, r, d), F32),
            jax.ShapeDtypeStruct((m * d // (2 * LANES), LANES), jnp.uint32),
            jax.ShapeDtypeStruct((N_EXPERTS, m), F32),
        ],
        compiler_params=_cparams(("parallel",), 48),
        name=name,
    )(x3, oa, ob, gb, ga, gm, mod3, mod3, mod3, nw2, gnw, wpa, wpb, wout, wr_t)


def _route_kernel(lg_ref, br_ref, idx_ref, gw_ref, rank_ref, cnt_ref, carry_ref):
    @pl.when(pl.program_id(0) == 0)
    def _():
        carry_ref[...] = jnp.zeros(carry_ref.shape, F32)

    logits = lg_ref[...]
    n_e, tm = logits.shape
    scores = jax.nn.sigmoid(logits)
    biased = scores + jnp.concatenate([br_ref[...]] * (tm // LANES), axis=1)
    g3 = biased.reshape(N_GROUPS, GROUP_SIZE, tm)
    sub = lax.broadcasted_iota(I32, g3.shape, 1)
    m1 = jnp.max(g3, axis=1, keepdims=True)
    f1 = jnp.min(jnp.where(g3 == m1, sub, GROUP_SIZE), axis=1, keepdims=True)
    m2 = jnp.max(jnp.where(sub == f1, -jnp.inf, g3), axis=1, keepdims=True)
    gs = (m1 + m2).reshape(N_GROUPS, tm)
    giota = lax.broadcasted_iota(I32, gs.shape, 0)
    gmask = jnp.zeros(gs.shape, jnp.bool_)
    for _ in range(TOPK_GROUPS):
        first = _first_argmax(gs, giota, 0, N_GROUPS)
        hit = giota == first
        gmask = gmask | hit
        gs = jnp.where(hit, -jnp.inf, gs)
    emask = jnp.broadcast_to(gmask.reshape(N_GROUPS, 1, tm), (N_GROUPS, GROUP_SIZE, tm)).reshape(n_e, tm)
    masked = jnp.where(emask, biased, -jnp.inf)
    eiota = lax.broadcasted_iota(I32, masked.shape, 0)
    idx_rows, w_rows = [], []
    for _ in range(TOP_K):
        first = _first_argmax(masked, eiota, 0, n_e)
        hit = eiota == first
        idx_rows.append(first)
        w_rows.append(jnp.sum(jnp.where(hit, scores, 0.0), axis=0, keepdims=True))
        masked = jnp.where(hit, -jnp.inf, masked)
    w = jnp.concatenate(w_rows, axis=0)
    idx_ref[...] = jnp.concatenate(idx_rows, axis=0)
    gw_ref[...] = w / jnp.sum(w, axis=0, keepdims=True) * ROUTED_SCALE
    chosen = jnp.zeros(masked.shape, F32)
    for first in idx_rows:
        chosen = chosen + (eiota == first).astype(F32)
    earlier = (lax.broadcasted_iota(I32, (tm, tm), 0) < lax.broadcasted_iota(I32, (tm, tm), 1)).astype(BF16)
    carry = carry_ref[...]
    base = _dot(chosen.astype(BF16), earlier) + jnp.concatenate([carry] * (tm // LANES), axis=1)
    rank_ref[...] = jnp.concatenate(
        [jnp.sum(jnp.where(eiota == first, base, 0.0), axis=0, keepdims=True) for first in idx_rows], axis=0).astype(I32)
    carry = carry + jnp.sum(chosen, axis=1, keepdims=True)
    carry_ref[...] = carry
    cnt_ref[...] = carry.astype(I32)


def _route_call(logits_t, b_router_lanes):
    n_e, m = logits_t.shape
    tm = 512 if m % 512 == 0 else ROW_TILE
    tok = lambda dt: jax.ShapeDtypeStruct((TOP_K, m), dt)
    tok_spec = pl.BlockSpec((TOP_K, tm), lambda i: (0, i))
    return pl.pallas_call(
        _route_kernel,
        grid=(m // tm,),
        in_specs=[pl.BlockSpec((n_e, tm), lambda i: (0, i)), pl.BlockSpec((n_e, LANES), lambda i: (0, 0))],
        out_specs=[tok_spec, tok_spec, tok_spec, pl.BlockSpec((n_e, LANES), lambda i: (0, 0))],
        out_shape=[tok(I32), tok(F32), tok(I32), jax.ShapeDtypeStruct((n_e, LANES), I32)],
        scratch_shapes=[pltpu.VMEM((n_e, LANES), F32)],
        compiler_params=_cparams(("arbitrary",), 32),
        name="moe_route",
    )(logits_t, b_router_lanes)


DMA_ISSUE_UNROLL = 8


def _dispatch_kernel(cnt_ref, pstart_ref, nu_ref, pos_ref, h2g_ref, xs_hbm, zero_ref, sem, zsem, gsem, *, tm, sub):
    i = pl.program_id(0)
    group_sub = MOE_GROUP_ROWS * sub

    @pl.when(i == 0)
    def _():
        zero_ref[...] = jnp.zeros(zero_ref.shape, zero_ref.dtype)
        zero_row = zero_ref.at[pl.ds(0, sub), :]

        def drain(r, carry):
            pltpu.make_async_copy(zero_row, xs_hbm.at[pl.ds(0, sub), :], zsem).wait()
            return carry

        def per_expert(e, n_prev):
            count = cnt_ref[e]
            padded = (count + MOE_GROUP_ROWS - 1) // MOE_GROUP_ROWS * MOE_GROUP_ROWS
            base = pstart_ref[e]

            def fill(r, carry):
                dst = xs_hbm.at[pl.ds(pl.multiple_of((base + r) * sub, sub), sub), :]
                pltpu.make_async_copy(zero_row, dst, zsem).start()
                return carry

            lax.fori_loop(count, padded, fill, 0)
            lax.fori_loop(0, n_prev, drain, 0)
            return padded - count

        n_zero = lax.fori_loop(0, N_EXPERTS, per_expert, 0)
        n_groups = xs_hbm.shape[0] // group_sub

        def fill_group(g, carry):
            dst = xs_hbm.at[pl.ds(pl.multiple_of(g * group_sub, group_sub), group_sub), :]
            pltpu.make_async_copy(zero_ref, dst, gsem).start()
            return carry

        lax.fori_loop(nu_ref[0], n_groups, fill_group, 0)
        lax.fori_loop(0, n_zero, drain, 0)

        def drain_group(g, carry):
            pltpu.make_async_copy(zero_ref, xs_hbm.at[pl.ds(0, group_sub), :], gsem).wait()
            return carry

        lax.fori_loop(nu_ref[0], n_groups, drain_group, 0)

    def issue(tb, carry):
        for u in range(DMA_ISSUE_UNROLL):
            t = tb * DMA_ISSUE_UNROLL + u
            src = h2g_ref.at[pl.ds(pl.multiple_of(t * sub, sub), sub), :]
            for k in range(TOP_K):
                p = pl.multiple_of(pos_ref[k * tm + t], sub)
                pltpu.make_async_copy(src, xs_hbm.at[pl.ds(p, sub), :], sem).start(priority=k % 2)
        return carry

    lax.fori_loop(0, tm // DMA_ISSUE_UNROLL, issue, 0)
    for _ in range(TOP_K):
        pltpu.make_async_copy(h2g_ref, xs_hbm.at[pl.ds(0, tm * sub), :], sem).wait()


def _dispatch_call(counts, pad_start, n_used, pos_tiles, h2g, n_rows_sorted, sub):
    tm = ROW_TILE
    m = h2g.shape[0] // sub
    grid_spec = pltpu.PrefetchScalarGridSpec(
        num_scalar_prefetch=3,
        grid=(m // tm,),
        in_specs=[
            pl.BlockSpec((TOP_K * tm,), lambda i, c, p, u: (i,), memory_space=pltpu.SMEM),
            pl.BlockSpec((tm * sub, LANES), lambda i, c, p, u: (i, 0)),
        ],
        out_specs=pl.BlockSpec(memory_space=pl.ANY),
        scratch_shapes=[pltpu.VMEM((MOE_GROUP_ROWS * sub, LANES), h2g.dtype), pltpu.SemaphoreType.DMA(()),
                        pltpu.SemaphoreType.DMA(()), pltpu.SemaphoreType.DMA(())],
    )
    return pl.pallas_call(
        functools.partial(_dispatch_kernel, tm=tm, sub=sub),
        grid_spec=grid_spec,
        out_shape=jax.ShapeDtypeStruct((n_rows_sorted * sub, LANES), h2g.dtype),
        compiler_params=_cparams(("arbitrary",), 32),
        name="moe_dispatch",
    )(counts, pad_start, n_used, pos_tiles, h2g)


def _expert_kernel(ge_ref, nu_ref, x_ref, w1_ref, w3_ref, w2_ref, y_ref):
    g = pl.program_id(0)
    rows = MOE_GROUP_ROWS

    @pl.when(g < nu_ref[0])
    def _():
        xb = _unpack_bf16_pairs(_load_row_tiles(x_ref, rows, x_ref.shape[0] // rows))
        hidden = _silu(_dot(xb, w1_ref[0])) * _dot(xb, w3_ref[0])
        _store_row_tiles(y_ref, _dot(hidden.astype(BF16), w2_ref[0]))

    @pl.when(g >= nu_ref[0])
    def _():
        y_ref[...] = jnp.zeros(y_ref.shape, F32)


def _expert_call(group_expert, n_used, xs, w1, w3, w2):
    n_groups = group_expert.shape[0]
    rows = MOE_GROUP_ROWS
    _, d, hidden = w1.shape
    sub = d // LANES
    used = lambda g, nu: jnp.minimum(g, nu[0] - 1)
    grid_spec = pltpu.PrefetchScalarGridSpec(
        num_scalar_prefetch=2,
        grid=(n_groups,),
        in_specs=[
            pl.BlockSpec((xs.shape[0] // n_groups, LANES), lambda g, ge, nu: (used(g, nu), 0)),
            pl.BlockSpec((1, d, hidden), lambda g, ge, nu: (ge[used(g, nu)], 0, 0)),
            pl.BlockSpec((1, d, hidden), lambda g, ge, nu: (ge[used(g, nu)], 0, 0)),
            pl.BlockSpec((1, hidden, d), lambda g, ge, nu: (ge[used(g, nu)], 0, 0)),
        ],
        out_specs=pl.BlockSpec((rows * sub, LANES), lambda g, ge, nu: (g, 0)),
    )
    return pl.pallas_call(
        _expert_kernel,
        grid_spec=grid_spec,
        out_shape=jax.ShapeDtypeStruct((n_groups * rows * sub, LANES), F32),
        compiler_params=_cparams(("arbitrary",), 40),
        name="moe_experts",
    )(group_expert, n_used, xs, w1, w3, w2)


def _start_tile_gather(idx_ref, n_rows, sub, src_hbm, dst, sem):
    def issue(rb, carry):
        for u in range(DMA_ISSUE_UNROLL):
            r = rb * DMA_ISSUE_UNROLL + u
            p = pl.multiple_of(idx_ref[r], sub)
            pltpu.make_async_copy(src_hbm.at[pl.ds(p, sub), :],
                                  dst.at[pl.ds(pl.multiple_of(r * sub, sub), sub), :], sem).start(priority=u % 2)
        return carry

    lax.fori_loop(0, n_rows // DMA_ISSUE_UNROLL, issue, 0)


def _final_kernel(pos_ref, pos_next_ref, gw_ref, x1_ref, h2g_ref, g2_ref, nwf_ref, ws1_ref, ws3_ref, ws2_ref, ys_hbm,
                  y_ref, buf, sem):
    i = pl.program_id(0)
    n_i = pl.num_programs(0)
    g, r, d = x1_ref.shape
    tm = g * r
    sub = d // LANES
    n_rows = TOP_K * tm
    slot = i % 2

    @pl.when(i == 0)
    def _():
        _start_tile_gather(pos_ref, n_rows, sub, ys_hbm, buf.at[0], sem.at[0])

    @pl.when(i + 1 < n_i)
    def _():
        _start_tile_gather(pos_next_ref, n_rows, sub, ys_hbm, buf.at[1 - slot], sem.at[1 - slot])

    hb = _unpack_bf16_pairs(_load_row_tiles(h2g_ref, tm, h2g_ref.shape[0] // tm))
    shared = _dot((_silu(_dot(hb, ws1_ref[...])) * _dot(hb, ws3_ref[...])).astype(BF16), ws2_ref[...])
    gw_t = jnp.concatenate([gw_ref[...], jnp.zeros((LANES - TOP_K, tm), F32)], axis=0).T
    gates = [jnp.broadcast_to(gw_t[:, k:k + 1], (tm, LANES)) for k in range(TOP_K)]
    pltpu.make_async_copy(ys_hbm.at[pl.ds(0, n_rows * sub), :], buf.at[slot], sem.at[slot]).wait()
    parts = []
    for s in range(sub):
        acc = None
        for k in range(TOP_K):
            term = buf[slot, pl.ds(k * tm * sub + s, tm, stride=sub), :] * gates[k]
            acc = term if acc is None else acc + term
        parts.append(acc)
    routed = jnp.concatenate(parts, axis=1)
    x2 = x1_ref[...] + g2_ref[...] * (shared + routed).reshape(g, r, d)
    ms = jnp.mean(x2 * x2, axis=-1, keepdims=True)
    y_ref[...] = x2 * lax.rsqrt(ms + EPS) * nwf_ref[...]


def _final_call(pos_tiles, gw, x3, h2g, mod3, nwf, ws1, ws3, ws2, ysorted, *, groups_per_tile, mod_of_tile,
                first_tile, name):
    n_groups, r, d = x3.shape
    tm = groups_per_tile * r
    sub = d // LANES
    n_tiles = n_groups // groups_per_tile
    n_rows = TOP_K * tm
    const = lambda a: pl.BlockSpec(a.shape, lambda i: (0,) * a.ndim, pipeline_mode=pl.Buffered(1))
    return pl.pallas_call(
        _final_kernel,
        grid=(n_tiles,),
        in_specs=[
            pl.BlockSpec((n_rows,), lambda i: (first_tile + i,), memory_space=pltpu.SMEM),
            pl.BlockSpec((n_rows,), lambda i: (first_tile + jnp.minimum(i + 1, n_tiles - 1),),
                         memory_space=pltpu.SMEM),
            pl.BlockSpec((TOP_K, tm), lambda i: (0, first_tile + i)),
            pl.BlockSpec((groups_per_tile, r, d), lambda i: (i, 0, 0)),
            pl.BlockSpec((tm * (h2g.shape[0] * TOP_K // pos_tiles.shape[0]), LANES), lambda i: (first_tile + i, 0)),
            pl.BlockSpec((groups_per_tile, 1, d), lambda i: (mod_of_tile(i), 0, 5)),
            const(nwf), const(ws1), const(ws3), const(ws2),
            pl.BlockSpec(memory_space=pl.ANY),
        ],
        out_specs=pl.BlockSpec((groups_per_tile, r, d), lambda i: (i, 0, 0)),
        out_shape=jax.ShapeDtypeStruct((n_groups, r, d), F32),
        scratch_shapes=[pltpu.VMEM((2, n_rows * sub, LANES), F32), pltpu.SemaphoreType.DMA((2,))],
        compiler_params=_cparams(("arbitrary",), 56),
        name=name,
    )(pos_tiles, pos_tiles, gw, x3, h2g, mod3, nwf, ws1, ws3, ws2, ysorted)


def _paged_attn_kernel(pt_ref, ck_hbm, cv_hbm, q_ref, kn_ref, vn_ref, o_ref,
                       cbuf, sem, qbd_ref, s_ref, p_ref, l_ref, acc_ref, *, pages_per_seq, n_blocks, n_sel):
    n = pl.program_id(0)
    u = pl.program_id(1)
    n_seq = pl.num_programs(0)
    n_u = pl.num_programs(1)
    n_kc = n_u // 2
    _, pages_step, heads, hd, page = cbuf.shape
    width = heads * hd
    rows_step = pages_step * page
    blocks_step = rows_step // MOBA_BLOCK
    t_new = q_ref.shape[0]
    n_cols = heads * t_new
    slot = (n * n_u + u) % 2

    def start_chunk(nn, uu, s):
        def copy_pages(src_hbm, cc):
            for p in range(pages_step):
                page_id = pt_ref[nn * pages_per_seq + cc * pages_step + p]
                pltpu.make_async_copy(src_hbm.at[page_id], cbuf.at[s, p], sem.at[s]).start()

        @pl.when(uu < n_kc)
        def _():
            copy_pages(ck_hbm, uu)

        @pl.when(uu >= n_kc)
        def _():
            copy_pages(cv_hbm, uu - n_kc)

    @pl.when((n == 0) & (u == 0))
    def _():
        start_chunk(n, u, slot)

    last_u = u == n_u - 1

    @pl.when(jnp.logical_not(last_u & (n == n_seq - 1)))
    def _():
        start_chunk(jnp.where(last_u, n + 1, n), jnp.where(last_u, 0, u + 1), 1 - slot)

    @pl.when(u == 0)
    def _():
        q = q_ref[...] * (A_HEAD_DIM ** -0.5)
        qt = jnp.concatenate([q] * heads, axis=0)
        r_i = lax.broadcasted_iota(I32, qt.shape, 0)
        c_i = lax.broadcasted_iota(I32, qt.shape, 1)
        qbd = jnp.where(c_i // hd == r_i // t_new, qt, 0.0)
        hi = qbd.astype(BF16)
        qbd_ref[...] = jnp.concatenate([hi, (qbd - hi.astype(F32)).astype(BF16)], axis=0)

    def logits(k_t):
        s2 = _dot(qbd_ref[...], k_t)
        return s2[0:n_cols] + s2[n_cols:2 * n_cols]

    pltpu.make_async_copy(ck_hbm.at[pl.ds(0, pages_step)], cbuf.at[slot], sem.at[slot]).wait()
    chunk_t = jnp.concatenate([cbuf[slot, p].reshape(width, page) for p in range(pages_step)], axis=1).astype(BF16)

    @pl.when(u < n_kc)
    def _():
        s_ref[u] = logits(chunk_t)

    def block_logits(ref, j):
        c, off = divmod(j, blocks_step)
        return ref[c, :, off * MOBA_BLOCK:(off + 1) * MOBA_BLOCK]

    @pl.when(u == n_kc - 1)
    def _():
        lane = lax.broadcasted_iota(I32, (n_cols, LANES), 1)
        sc = jnp.full((n_cols, LANES), NEG, F32)
        for j in range(n_blocks):
            sc = jnp.where(lane == j, jnp.sum(block_logits(s_ref, j), axis=1, keepdims=True), sc)
        sel = jnp.zeros(sc.shape, jnp.bool_)
        for _ in range(n_sel):
            first = _first_argmax(sc, lane, 1, LANES)
            hit = lane == first
            sel = sel | hit
            sc = jnp.where(hit, BELOW_NEG, sc)
        keep = jnp.where(sel, 0.0, NEG)
        bias = [jnp.max(jnp.where(lane == j, keep, NEG), axis=1, keepdims=True) for j in range(n_blocks)]
        s_new = lax.dot_general(qbd_ref[...], kn_ref[...].astype(BF16), (((1,), (1,)), ((), ())),
                                preferred_element_type=F32)
        s_new = s_new[0:n_cols] + s_new[n_cols:2 * n_cols]
        r_i = lax.broadcasted_iota(I32, s_new.shape, 0)
        c_i = lax.broadcasted_iota(I32, s_new.shape, 1)
        s_new = jnp.where(c_i <= r_i % t_new, s_new, NEG)
        m = jnp.max(s_new, axis=1, keepdims=True)
        for j in range(n_blocks):
            m = jnp.maximum(m, jnp.max(block_logits(s_ref, j) + bias[j], axis=1, keepdims=True))
        p_new = jnp.exp(s_new - m)
        l = jnp.sum(p_new, axis=1, keepdims=True)
        for j in range(n_blocks):
            pj = jnp.exp(block_logits(s_ref, j) + bias[j] - m)
            l = l + jnp.sum(pj, axis=1, keepdims=True)
            c, off = divmod(j, blocks_step)
            p_ref[c, :, off * MOBA_BLOCK:(off + 1) * MOBA_BLOCK] = pj.astype(BF16)
        l_ref[...] = jnp.broadcast_to(l, l_ref.shape)
        acc_ref[...] = _dot(p_new.astype(BF16), vn_ref[...].astype(BF16))

    @pl.when(u >= n_kc)
    def _():
        acc_ref[...] += _dot_nt(p_ref[u - n_kc], chunk_t)

    @pl.when(last_u)
    def _():
        o_ref[0] = acc_ref[...] / l_ref[:, 0:1]


def _paged_attn_call(page_table_flat, cache_k4, cache_v4, q, k_new, v_new, n_seq, pages_per_seq, t_new):
    _, heads, hd, page = cache_k4.shape
    width = heads * hd
    past = pages_per_seq * page
    n_blocks = past // MOBA_BLOCK
    pages_step = PAGED_ATTN_PAGES_PER_STEP
    while pages_per_seq % pages_step:
        pages_step //= 2
    rows_step = pages_step * page
    n_cols = heads * t_new
    assert rows_step % MOBA_BLOCK == 0 and 2 * n_cols <= LANES and n_blocks <= LANES
    n_kc = pages_per_seq // pages_step
    n_sel = min(MOBA_TOPK, n_blocks)
    new_spec = pl.BlockSpec((t_new, width), lambda n, u, pt: (n, 0))
    grid_spec = pltpu.PrefetchScalarGridSpec(
        num_scalar_prefetch=1,
        grid=(n_seq, 2 * n_kc),
        in_specs=[pl.BlockSpec(memory_space=pl.ANY), pl.BlockSpec(memory_space=pl.ANY), new_spec, new_spec, new_spec],
        out_specs=pl.BlockSpec((1, n_cols, width), lambda n, u, pt: (n, 0, 0)),
        scratch_shapes=[
            pltpu.VMEM((2, pages_step, heads, hd, page), F32),
            pltpu.SemaphoreType.DMA((2,)),
            pltpu.VMEM((2 * n_cols, width), BF16),
            pltpu.VMEM((n_kc, n_cols, rows_step), F32),
            pltpu.VMEM((n_kc, n_cols, rows_step), BF16),
            pltpu.VMEM((n_cols, LANES), F32),
            pltpu.VMEM((n_cols, width), F32),
        ],
    )
    return pl.pallas_call(
        functools.partial(_paged_attn_kernel, pages_per_seq=pages_per_seq, n_blocks=n_blocks, n_sel=n_sel),
        grid_spec=grid_spec,
        out_shape=jax.ShapeDtypeStruct((n_seq, n_cols, width), F32),
        compiler_params=_cparams(("arbitrary", "arbitrary"), 56),
        name="sample_attention",
    )(page_table_flat, cache_k4, cache_v4, q, k_new, v_new)


def _rope_tables(pos):
    inv_freq = ROPE_THETA ** (-jnp.arange(0, A_HEAD_DIM, 2, dtype=F32) / A_HEAD_DIM)
    ang = pos.astype(F32)[:, None] * inv_freq[None, :]
    cos = jnp.concatenate([jnp.cos(ang)] * (2 * LANES // A_HEAD_DIM), axis=-1)
    sin = jnp.concatenate([jnp.sin(ang)] * (2 * LANES // A_HEAD_DIM), axis=-1)
    return cos, sin


def _dispatch_tables(idx, rank, counts):
    k, m = idx.shape
    rows_g = MOE_GROUP_ROWS
    padded = (counts + rows_g - 1) // rows_g * rows_g
    pad_end = jnp.cumsum(padded)
    pad_start = pad_end - padded
    n_groups = -(-(k * m) // rows_g) + N_EXPERTS
    onehot = idx[:, :, None] == jnp.arange(N_EXPERTS, dtype=I32)[None, None, :]
    pos = jnp.sum(jnp.where(onehot, pad_start[None, None, :], 0), axis=-1) + rank
    group_start = jnp.arange(n_groups, dtype=I32) * rows_g
    group_expert = jnp.minimum(jnp.sum(pad_end[None, :] <= group_start[:, None], axis=1), N_EXPERTS - 1).astype(I32)
    n_used = (pad_end[-1:] // rows_g).astype(I32)
    pos_tiles = pos.reshape(k, m // ROW_TILE, ROW_TILE).transpose(1, 0, 2).reshape(-1).astype(I32)
    return group_expert, n_used, pad_start.astype(I32), pos_tiles, n_groups


def kernel(x_prompt, x_sample, cache_k, cache_v, state_gla, page_table, c_prompt, c_sample, w_ada, b_ada,
           norm_mix_w, norm_ffn_w, norm_final_w, w_in, w_gate_up, b_gate, gla_norm_w, w_proj_a, w_proj_b, w_out,
           w_router, b_router, w_e1, w_e3, w_e2, w_s1, w_s3, w_s2):
    batch, seq, d = x_prompt.shape
    n_dec, t_new, _ = x_sample.shape
    depth = w_ada.shape[0]
    n_pool, page = cache_k.shape[1], cache_k.shape[2]
    pages_per_seq = page_table.shape[1]
    past_len = pages_per_seq * page
    assert depth == 1 and past_len % MOBA_BLOCK == 0 and MOBA_BLOCK % page == 0
    assert seq % ROW_TILE == 0 and ROW_TILE % MOBA_BLOCK == 0 and ROW_TILE % t_new == 0 and t_new == SUBLANES
    assert seq % ATTN_KEY_CHUNK == 0 and ATTN_KEY_CHUNK % MOBA_BLOCK == 0 and (m_all := batch * seq + n_dec * t_new) % ROW_TILE == 0
    del m_all
    m_p, m_s = batch * seq, n_dec * t_new
    bkw = w_gate_up.shape[2]
    dk = bkw // B_HEADS
    dv = gla_norm_w.shape[1]
    bvw = B_HEADS * dv

    n_c = batch + n_dec
    n_c_pad = -(-n_c // SUBLANES) * SUBLANES
    c_all = jnp.concatenate([c_prompt, c_sample, jnp.zeros((n_c_pad - n_c, d), F32)], axis=0)
    mod = _ada_call(c_all, w_ada[0], b_ada[0])
    mod_p = mod[:batch].reshape(batch, 1, 6 * d)
    mod_s = mod[batch:n_c].reshape(n_dec, 1, 6 * d)

    sizes = (A_WIDTH, A_WIDTH, A_WIDTH, bkw, bkw, bvw, bvw, GATE_RANK, d, d)
    offs = [0]
    for s in sizes:
        offs.append(offs[-1] + s)
    w_in0 = w_in[0]
    w_main = jnp.concatenate([w_in0[:, :offs[7]], w_in0[:, offs[8]:]], axis=1).astype(BF16)
    w_gr = jnp.pad(w_in0[:, offs[7]:offs[8]], ((0, 0), (0, LANES - GATE_RANK))).astype(BF16)
    wg = jnp.pad(w_gate_up[0], ((0, LANES - GATE_RANK), (0, 0)))
    bg = b_gate[0].reshape(1, bkw)
    nw1 = norm_mix_w[0].reshape(1, d)
    nw2 = norm_ffn_w[0].reshape(1, d)
    nwf = norm_final_w.reshape(1, d)
    gnw = gla_norm_w[0].reshape(1, dv)
    wpa = w_proj_a[0].astype(BF16)
    wpb = w_proj_b[0].astype(BF16)
    wout = w_out[0].astype(BF16)
    wr_t = w_router[0].T
    br_lanes = jnp.broadcast_to(b_router[0][:, None], (N_EXPERTS, LANES))
    we1, we3, we2 = w_e1[0].astype(BF16), w_e3[0].astype(BF16), w_e2[0].astype(BF16)
    ws1, ws3, ws2 = w_s1[0].astype(BF16), w_s3[0].astype(BF16), w_s2[0].astype(BF16)

    tiles_per_seq = seq // ROW_TILE
    groups_s = ROW_TILE // t_new

    cos_p, sin_p = _rope_tables(jnp.arange(seq, dtype=I32))
    xp3 = x_prompt.reshape(m_p // ROW_TILE, ROW_TILE, d)
    (qa_p, ka_p, va_p, qb_p, kb_p, vb_p, gb_p, ga_p, gm_p, gr_p, kmean_p) = _inproj_call(
        xp3, mod_p, nw1, cos_p, sin_p, w_main, w_gr, groups_per_tile=1, rows_per_group=ROW_TILE,
        mod_of_tile=lambda i: i // tiles_per_seq, cos_of_tile=lambda i: i % tiles_per_seq, with_kmean=True,
        name="inproj_prompt")
    qaug, kaug, vaug = _select_call(qa_p, ka_p, va_p, kmean_p, batch, seq)
    oa_p = _moba_attn_call(qaug, kaug, vaug).reshape(m_p, A_WIDTH)
    ob_p, state_p = _gla_call(qb_p, kb_p, vb_p, gr_p, wg, bg, None, n_seq=batch, seq_len=seq, chunk=GLA_CHUNK,
                              chunks_per_step=GLA_CHUNKS_PER_STEP, name="gla_prompt")
    x1_p, h2_p, lg_p = _mixer_call(
        xp3, oa_p, ob_p, gb_p, ga_p, gm_p, mod_p, nw2, gnw, wpa, wpb, wout, wr_t, groups_per_tile=1,
        mod_of_tile=lambda i: i // tiles_per_seq, name="mixer_prompt")

    cos_s, sin_s = _rope_tables(past_len + jnp.arange(t_new, dtype=I32))
    cos_s = jnp.tile(cos_s, (groups_s, 1))
    sin_s = jnp.tile(sin_s, (groups_s, 1))
    (qa_s, ka_s, va_s, qb_s, kb_s, vb_s, gb_s, ga_s, gm_s, gr_s) = _inproj_call(
        x_sample, mod_s, nw1, cos_s, sin_s, w_main, w_gr, groups_per_tile=groups_s, rows_per_group=t_new,
        mod_of_tile=lambda i: i, cos_of_tile=lambda i: 0, with_kmean=False, name="inproj_sample")
    pt_flat = page_table.reshape(-1)
    ck4 = cache_k[0].transpose(0, 2, 3, 1)
    cv4 = cache_v[0].transpose(0, 2, 3, 1)
    o_c = _paged_attn_call(pt_flat, ck4, cv4, qa_s, ka_s, va_s, n_dec, pages_per_seq, t_new)
    o5 = o_c.reshape(n_dec, A_HEADS, t_new, A_HEADS, A_HEAD_DIM)
    hh = jnp.arange(A_HEADS)
    oa_s = o5[:, hh, :, hh, :].transpose(1, 2, 0, 3).reshape(m_s, A_WIDTH)
    ob_s, state_s = _gla_call(qb_s, kb_s, vb_s, gr_s, wg, bg, state_gla[0], n_seq=n_dec, seq_len=t_new, chunk=t_new,
                              chunks_per_step=1, name="gla_sample")
    x1_s, h2_s, lg_s = _mixer_call(
        x_sample, oa_s, ob_s, gb_s, ga_s, gm_s, mod_s, nw2, gnw, wpa, wpb, wout, wr_t, groups_per_tile=groups_s,
        mod_of_tile=lambda i: i, name="mixer_sample")

    sub_x = d // (2 * LANES)
    sub_y = d // LANES
    h2g = jnp.concatenate([h2_p, h2_s], axis=0)
    idx, gw, rank, cnt = _route_call(jnp.concatenate([lg_p, lg_s], axis=1), br_lanes)
    counts = cnt[:, 0]
    group_expert, n_used, pad_start, pos_tiles, n_groups = _dispatch_tables(idx, rank, counts)
    xsorted = _dispatch_call(counts, pad_start, n_used, pos_tiles * sub_x, h2g, n_groups * MOE_GROUP_ROWS, sub_x)
    ysorted = _expert_call(group_expert, n_used, xsorted, we1, we3, we2)
    n_tiles_p = m_p // ROW_TILE
    pos_y = pos_tiles * sub_y
    y_p = _final_call(pos_y, gw, x1_p, h2g, mod_p, nwf, ws1, ws3, ws2, ysorted, groups_per_tile=1,
                      mod_of_tile=lambda i: i // tiles_per_seq, first_tile=0, name="final_prompt")
    y_s = _final_call(pos_y, gw, x1_s, h2g, mod_s, nwf, ws1, ws3, ws2, ysorted, groups_per_tile=groups_s,
                      mod_of_tile=lambda i: i, first_tile=n_tiles_p, name="final_sample")

    return (
        y_p.reshape(batch, seq, d),
        y_s,
        ka_p.reshape(1, batch, seq, A_HEADS, A_HEAD_DIM),
        va_p.reshape(1, batch, seq, A_HEADS, A_HEAD_DIM),
        state_p[None],
        ka_s.reshape(1, n_dec, t_new, A_HEADS, A_HEAD_DIM),
        va_s.reshape(1, n_dec, t_new, A_HEADS, A_HEAD_DIM),
        state_s[None],
    )
```
